```python
import math
import jax, jax.numpy as jnp
from jax import lax
import numpy as np

D_MODEL = 1024
BATCH = 4
SEQ = 4096
DEPTH = 4
DEC_BATCH = 16
DEC_SEQ = 16
PAST_LEN = 1024

CHUNK = 64
HEAD_DIM = 64
D_MIX = D_MODEL
D_FOX = D_MIX // 4
H_FOX = D_FOX // HEAD_DIM
D_BAND = D_MIX // 4
H_BAND = D_BAND // HEAD_DIM
D_LRU = D_MIX // 2
LRU_BLOCKS = 8
LRU_BW = D_LRU // LRU_BLOCKS
LRU_CONV = 4
RGLRU_C = 8.0
BAND_PREV = 8
BAND_LEN = (BAND_PREV + 1) * CHUNK
BAND_KEEP = BAND_PREV * CHUNK
REL_CLIP = 128
N_REL = REL_CLIP + CHUNK
D_FF = 3 * D_MODEL
FFN_CONV = 3
FOX_QBLOCK = 128
RMS_EPS = 1e-6
NEG_INF = -1e30
ATTN_SCALE = HEAD_DIM ** -0.5
SPLIT_SIZES = (D_FOX, D_FOX, D_FOX, H_FOX, D_LRU, D_LRU, D_BAND, D_BAND, D_BAND)
N_IN = 3 * D_FOX + H_FOX + 2 * D_LRU + 3 * D_BAND

kernel_name = 'hybrid_streaming_encoder_step'


def rms_norm(x, g):
    xf = x.astype(jnp.float32)
    y = xf * lax.rsqrt(jnp.mean(xf * xf, axis=-1, keepdims=True) + RMS_EPS)
    return (y * g.astype(jnp.float32)).astype(x.dtype)


def causal_dwconv(x, hist, w, b):
    T = x.shape[1]
    width = w.shape[0]
    xp = jnp.concatenate([hist, x], axis=1)
    y = xp[:, 0:T] * w[0]
    for k in range(1, width):
        y = y + xp[:, k:k + T] * w[k]
    return y + b, xp[:, xp.shape[1] - (width - 1):]


def fox_attention(q, k_all, v_all, logf_all, n_hist):
    B, T, H, D = q.shape
    S = k_all.shape[1]
    qb = min(FOX_QBLOCK, T)
    nb = T // qb
    c = jnp.cumsum(logf_all.astype(jnp.float32), axis=1).transpose(0, 2, 1)
    q_blocks = q.reshape(B, nb, qb, H, D).transpose(1, 0, 2, 3, 4)
    c_q = c[:, :, n_hist:].reshape(B, H, nb, qb).transpose(2, 0, 1, 3)
    k_pos = jnp.arange(S)

    def one_block(args):
        q_blk, cq_blk, blk = args
        s = jnp.einsum('bqhd,bkhd->bhqk', q_blk, k_all, preferred_element_type=jnp.float32) * ATTN_SCALE
        s = s + cq_blk[..., None] - c[:, :, None, :]
        q_pos = n_hist + blk * qb + jnp.arange(qb)
        s = jnp.where(k_pos[None, :] <= q_pos[:, None], s, NEG_INF)
        p = jax.nn.softmax(s, axis=-1).astype(v_all.dtype)
        return jnp.einsum('bhqk,bkhd->bqhd', p, v_all)

    out = lax.map(one_block, (q_blocks, c_q, jnp.arange(nb)))
    return out.transpose(1, 0, 2, 3, 4).reshape(B, T, H, D)


def band_attention(q, k_all, v_all, n_hist, rel_table):
    B, T, H, D = q.shape
    nq = -(-T // CHUNK)
    tq = nq * CHUNK
    front = BAND_KEEP - n_hist
    pad = ((0, 0), (front, tq - T), (0, 0), (0, 0))
    n_kc = BAND_PREV + nq
    k_ch = jnp.pad(k_all, pad).reshape(B, n_kc, CHUNK, H, D)
    v_ch = jnp.pad(v_all, pad).reshape(B, n_kc, CHUNK, H, D)
    gidx = jnp.arange(nq)[:, None] + jnp.arange(BAND_PREV + 1)[None, :]
    k_band = k_ch[:, gidx].reshape(B, nq, BAND_LEN, H, D)
    v_band = v_ch[:, gidx].reshape(B, nq, BAND_LEN, H, D)
    q_ch = jnp.pad(q, ((0, 0), (0, tq - T), (0, 0), (0, 0))).reshape(B, nq, CHUNK, H, D)
    s = jnp.einsum('bnqhd,bnkhd->bnhqk', q_ch, k_band, preferred_element_type=jnp.float32) * ATTN_SCALE
    i = jnp.arange(CHUNK)
    j = jnp.arange(BAND_LEN)
    dist = j[None, :] - BAND_KEEP - i[:, None]
    ridx = jnp.clip(dist, -REL_CLIP, CHUNK - 1) + REL_CLIP
    bias = rel_table[:, ridx].astype(jnp.float32)
    gpos = jnp.arange(nq)[:, None] * CHUNK + j[None, :]
    valid = (gpos >= front) & (gpos < front + n_hist + T)
    s = jnp.where(valid[None, :, None, None, :], s + bias[None, None], NEG_INF)
    p = jax.nn.softmax(s, axis=-1).astype(v_all.dtype)
    o = jnp.einsum('bnhqk,bnkhd->bnqhd', p, v_band)
    return o.reshape(B, tq, H, D)[:, :T]


def _linear_combine(e1, e2):
    a1, b1 = e1
    a2, b2 = e2
    return a1 * a2, a2 * b1 + b2


def rglru_block(xb, yb, conv_hist, h0, conv_w, conv_b, w_a, b_a, w_x, b_x, lam):
    B, T, C = xb.shape
    xc, conv_new = causal_dwconv(xb, conv_hist, conv_w, conv_b)
    xg = xc.reshape(B, T, LRU_BLOCKS, LRU_BW)
    r = jax.nn.sigmoid((jnp.einsum('btnd,nde->btne', xg, w_a).reshape(B, T, C) + b_a).astype(jnp.float32))
    ig = jax.nn.sigmoid((jnp.einsum('btnd,nde->btne', xg, w_x).reshape(B, T, C) + b_x).astype(jnp.float32))
    log_a = RGLRU_C * r * jax.nn.log_sigmoid(lam.astype(jnp.float32))
    a = jnp.exp(log_a)
    u = jnp.sqrt(-jnp.expm1(2.0 * log_a)) * (ig * xc.astype(jnp.float32))
    a_cum, h_part = lax.associative_scan(_linear_combine, (a, u), axis=1)
    h = a_cum * h0.astype(jnp.float32)[:, None, :] + h_part
    y = h.astype(xb.dtype) * jax.nn.gelu(yb)
    return y, h[:, -1].astype(xb.dtype), conv_new


def trunk_layer(x, fox_k_h, fox_v_h, fox_lf_h, band_k_h, band_v_h, lru_h0, lru_conv_h, ffn_conv_h,
                norm_g, w_in, b_forget, lru_conv_w, lru_conv_b, lru_wa, lru_ba, lru_wx, lru_bx,
                lru_lambda, rel_bias, w_out, w_up, ffn_conv_w, ffn_conv_b, w_down):
    B, T, _ = x.shape
    split_points = np.cumsum(SPLIT_SIZES)[:-1].tolist()
    h = rms_norm(x, norm_g[0])
    qa, ka, va, fa, xb, yb, qc, kc, vc = jnp.split(h @ w_in, split_points, axis=-1)
    qa = qa.reshape(B, T, H_FOX, HEAD_DIM)
    ka = ka.reshape(B, T, H_FOX, HEAD_DIM)
    va = va.reshape(B, T, H_FOX, HEAD_DIM)
    qc = qc.reshape(B, T, H_BAND, HEAD_DIM)
    kc = kc.reshape(B, T, H_BAND, HEAD_DIM)
    vc = vc.reshape(B, T, H_BAND, HEAD_DIM)
    logf = jax.nn.log_sigmoid(fa.astype(jnp.float32) + b_forget.astype(jnp.float32)).astype(x.dtype)

    o_fox = fox_attention(qa, jnp.concatenate([fox_k_h, ka], axis=1), jnp.concatenate([fox_v_h, va], axis=1),
                          jnp.concatenate([fox_lf_h, logf], axis=1), fox_k_h.shape[1])
    o_lru, h_last, lru_conv_new = rglru_block(xb, yb, lru_conv_h, lru_h0, lru_conv_w, lru_conv_b,
                                              lru_wa, lru_ba, lru_wx, lru_bx, lru_lambda)
    o_band = band_attention(qc, jnp.concatenate([band_k_h, kc], axis=1), jnp.concatenate([band_v_h, vc], axis=1),
                            band_k_h.shape[1], rel_bias)
    mix = jnp.concatenate([o_fox.reshape(B, T, D_FOX), o_lru, o_band.reshape(B, T, D_BAND)], axis=-1) @ w_out
    x = x + rms_norm(mix, norm_g[1])

    h2 = rms_norm(x, norm_g[2])
    gate, val = jnp.split(h2 @ w_up, 2, axis=-1)
    gate_c, ffn_conv_new = causal_dwconv(gate, ffn_conv_h, ffn_conv_w, ffn_conv_b)
    f = (jax.nn.gelu(gate_c) * val) @ w_down
    x = x + rms_norm(f, norm_g[3])

    keep = min(BAND_KEEP, T)
    return x, (ka, va, logf, kc[:, T - keep:], vc[:, T - keep:], h_last, lru_conv_new, ffn_conv_new)


def setup_inputs(seed: int = 0) -> dict:
    key = jax.random.key(seed)
    ks = jax.random.split(key, 26)
    f32 = jnp.float32

    def nrm(k, shape, scale):
        return jax.random.normal(k, shape, f32) * scale

    band_past = min(BAND_KEEP, PAST_LEN)
    u = jax.random.uniform(ks[19], (DEPTH, D_LRU), f32, 0.9, 0.999)
    s = u ** (1.0 / RGLRU_C)
    lam = jnp.log(s) - jnp.log1p(-s)
    return {
        'x_prompt': nrm(ks[0], (BATCH, SEQ, D_MODEL), 1.0),
        'x_sample': nrm(ks[1], (DEC_BATCH, DEC_SEQ, D_MODEL), 1.0),
        'cache_fox_k': nrm(ks[2], (DEPTH, DEC_BATCH, PAST_LEN, H_FOX, HEAD_DIM), 1.0),
        'cache_fox_v': nrm(ks[3], (DEPTH, DEC_BATCH, PAST_LEN, H_FOX, HEAD_DIM), 1.0),
        'cache_fox_logf': jax.nn.log_sigmoid(2.5 + nrm(ks[4], (DEPTH, DEC_BATCH, PAST_LEN, H_FOX), 1.0)),
        'cache_band_k': nrm(ks[5], (DEPTH, DEC_BATCH, band_past, H_BAND, HEAD_DIM), 1.0),
        'cache_band_v': nrm(ks[6], (DEPTH, DEC_BATCH, band_past, H_BAND, HEAD_DIM), 1.0),
        'state_lru_h': nrm(ks[7], (DEPTH, DEC_BATCH, D_LRU), 0.5),
        'state_lru_conv': nrm(ks[8], (DEPTH, DEC_BATCH, LRU_CONV - 1, D_LRU), 1.0),
        'state_ffn_conv': nrm(ks[9], (DEPTH, DEC_BATCH, FFN_CONV - 1, D_FF), 1.0),
        'norm_g': 1.0 + nrm(ks[10], (DEPTH, 4, D_MODEL), 0.05),
        'w_in': nrm(ks[11], (DEPTH, D_MODEL, N_IN), D_MODEL ** -0.5),
        'b_forget': jax.random.uniform(ks[12], (DEPTH, H_FOX), f32, 1.0, 4.0),
        'lru_conv_w': nrm(ks[13], (DEPTH, LRU_CONV, D_LRU), LRU_CONV ** -0.5),
        'lru_conv_b': nrm(ks[14], (DEPTH, D_LRU), 0.01),
        'lru_wa': nrm(ks[15], (DEPTH, LRU_BLOCKS, LRU_BW, LRU_BW), LRU_BW ** -0.5),
        'lru_ba': nrm(ks[16], (DEPTH, D_LRU), 0.01),
        'lru_wx': nrm(ks[17], (DEPTH, LRU_BLOCKS, LRU_BW, LRU_BW), LRU_BW ** -0.5),
        'lru_bx': nrm(ks[18], (DEPTH, D_LRU), 0.01),
        'lru_lambda': lam,
        'rel_bias': nrm(ks[20], (DEPTH, H_BAND, N_REL), 0.2),
        'w_out': nrm(ks[21], (DEPTH, D_MIX, D_MODEL), D_MIX ** -0.5),
        'w_up': nrm(ks[22], (DEPTH, D_MODEL, 2 * D_FF), D_MODEL ** -0.5),
        'ffn_conv_w': nrm(ks[23], (DEPTH, FFN_CONV, D_FF), FFN_CONV ** -0.5),
        'ffn_conv_b': nrm(ks[24], (DEPTH, D_FF), 0.01),
        'w_down': nrm(ks[25], (DEPTH, D_FF, D_MODEL), D_FF ** -0.5),
    }


def reference(x_prompt, x_sample, cache_fox_k, cache_fox_v, cache_fox_logf, cache_band_k, cache_band_v,
              state_lru_h, state_lru_conv, state_ffn_conv, norm_g, w_in, b_forget, lru_conv_w, lru_conv_b,
              lru_wa, lru_ba, lru_wx, lru_bx, lru_lambda, rel_bias, w_out, w_up, ffn_conv_w, ffn_conv_b, w_down):
    dt = x_prompt.dtype
    bp = x_prompt.shape[0]
    prompt_hist = (
        jnp.zeros((bp, 0, H_FOX, HEAD_DIM), dt),
        jnp.zeros((bp, 0, H_FOX, HEAD_DIM), dt),
        jnp.zeros((bp, 0, H_FOX), dt),
        jnp.zeros((bp, 0, H_BAND, HEAD_DIM), dt),
        jnp.zeros((bp, 0, H_BAND, HEAD_DIM), dt),
        jnp.zeros((bp, D_LRU), dt),
        jnp.zeros((bp, LRU_CONV - 1, D_LRU), dt),
        jnp.zeros((bp, FFN_CONV - 1, D_FF), dt),
    )
    xp, xs = x_prompt, x_sample
    st_p, st_s = [], []
    for l in range(DEPTH):
        params = (norm_g[l], w_in[l], b_forget[l], lru_conv_w[l], lru_conv_b[l], lru_wa[l], lru_ba[l],
                  lru_wx[l], lru_bx[l], lru_lambda[l], rel_bias[l], w_out[l], w_up[l], ffn_conv_w[l],
                  ffn_conv_b[l], w_down[l])
        sample_hist = (cache_fox_k[l], cache_fox_v[l], cache_fox_logf[l], cache_band_k[l], cache_band_v[l],
                       state_lru_h[l], state_lru_conv[l], state_ffn_conv[l])
        xp, new_p = trunk_layer(xp, *prompt_hist, *params)
        xs, new_s = trunk_layer(xs, *sample_hist, *params)
        st_p.append(new_p)
        st_s.append(new_s)

    def stack(states, j):
        return jnp.stack([st[j] for st in states], axis=0)

    fox_k_p, fox_k_s = stack(st_p, 0), stack(st_s, 0)
    fox_v_p, fox_v_s = stack(st_p, 1), stack(st_s, 1)
    fox_logf_p, fox_logf_s = stack(st_p, 2), stack(st_s, 2)
    band_k_p, band_k_s = stack(st_p, 3), stack(st_s, 3)
    band_v_p, band_v_s = stack(st_p, 4), stack(st_s, 4)
    lru_h_p, lru_h_s = stack(st_p, 5), stack(st_s, 5)
    lru_conv_p, lru_conv_s = stack(st_p, 6), stack(st_s, 6)
    ffn_conv_p, ffn_conv_s = stack(st_p, 7), stack(st_s, 7)
    return (xp, xs, fox_k_p, fox_k_s, fox_v_p, fox_v_s, fox_logf_p, fox_logf_s, band_k_p, band_k_s,
            band_v_p, band_v_s, lru_h_p, lru_h_s, lru_conv_p, lru_conv_s, ffn_conv_p, ffn_conv_s)
```

```python
import functools

import jax
import jax.numpy as jnp
from jax import lax
from jax.experimental import pallas as pl
from jax.experimental.pallas import tpu as pltpu

D_MODEL = 1024
CHUNK = 64
HEAD_DIM = 64
D_FOX = 256
H_FOX = 4
D_BAND = 256
H_BAND = 4
D_LRU = 512
LRU_BLOCKS = 8
LRU_BW = 64
LRU_CONV = 4
RGLRU_C = 8.0
BAND_PREV = 8
BAND_KEEP = BAND_PREV * CHUNK
REL_CLIP = 128
N_REL = REL_CLIP + CHUNK
D_FF = 3 * D_MODEL
FFN_CONV = 3
RMS_EPS = 1e-6
NEG_INF = -1e30
ATTN_SCALE = HEAD_DIM ** -0.5

LANES = 128
SUBLANES = 8
F_PAD = LANES
BAND_TQ = 2 * CHUNK
BAND_TK = BAND_TQ + BAND_KEEP
VMEM_LIMIT = 56 * 1024 * 1024

BF16 = jnp.bfloat16
F32 = jnp.float32


def _cparams(sem):
    return pltpu.CompilerParams(dimension_semantics=sem, vmem_limit_bytes=VMEM_LIMIT)


def _rms(x, g):
    y = x * lax.rsqrt(jnp.mean(x * x, axis=-1, keepdims=True) + RMS_EPS)
    return y * g


def _log_sigmoid(x):
    return jnp.minimum(x, 0.0) - jnp.log1p(jnp.exp(-jnp.abs(x)))


def _gelu(x):
    return jax.nn.gelu(x)


def _dot(a, b):
    return jnp.dot(a, b, preferred_element_type=F32)


def _dot_nt(a, b):
    return lax.dot_general(a, b, (((1,), (1,)), ((), ())), preferred_element_type=F32)


def _pre_kernel(x_ref, g_ref, wfox_ref, wf_ref, bf_ref, wlru_ref, wband_ref,
                qf_ref, kf_ref, vf_ref, kf16_ref, vf16_ref, lf_ref, lft_ref,
                xb_ref, yb_ref, qb_ref, kb_ref, vb_ref, kb16_ref, vb16_ref):
    h = _rms(x_ref[...], g_ref[...]).astype(BF16)
    fox = _dot(h, wfox_ref[...])
    qf_ref[...] = (fox[:, :D_FOX] * ATTN_SCALE).astype(BF16)
    k = fox[:, D_FOX:2 * D_FOX]
    v = fox[:, 2 * D_FOX:]
    kf_ref[...] = k
    vf_ref[...] = v
    kf16_ref[...] = k.astype(BF16)
    vf16_ref[...] = v.astype(BF16)
    lf = _log_sigmoid(_dot(h, wf_ref[...]) + bf_ref[...])
    lf_ref[...] = lf[:, :H_FOX]
    lft_ref[...] = lf.T[:SUBLANES, :]
    lru = _dot(h, wlru_ref[...])
    xb_ref[...] = lru[:, :D_LRU]
    yb_ref[...] = lru[:, D_LRU:]
    band = _dot(h, wband_ref[...])
    qb_ref[...] = (band[:, :D_BAND] * ATTN_SCALE).astype(BF16)
    k = band[:, D_BAND:2 * D_BAND]
    v = band[:, 2 * D_BAND:]
    kb_ref[...] = k
    vb_ref[...] = v
    kb16_ref[...] = k.astype(BF16)
    vb16_ref[...] = v.astype(BF16)


def _pre_call(x, g, wfox, wf, bf, wlru, wband, tm):
    n = x.shape[0]
    nt = n // tm
    row = lambda w: pl.BlockSpec((tm, w), lambda i: (i, 0))
    full = lambda a: pl.BlockSpec(a.shape, lambda i: (0,) * a.ndim)
    sd = jax.ShapeDtypeStruct
    out_shape = (
        sd((n, D_FOX), BF16), sd((n, D_FOX), F32), sd((n, D_FOX), F32),
        sd((n, D_FOX), BF16), sd((n, D_FOX), BF16),
        sd((n, H_FOX), F32), sd((SUBLANES, n), F32),
        sd((n, D_LRU), F32), sd((n, D_LRU), F32),
        sd((n, D_BAND), BF16), sd((n, D_BAND), F32), sd((n, D_BAND), F32),
        sd((n, D_BAND), BF16), sd((n, D_BAND), BF16),
    )
    out_specs = (
        row(D_FOX), row(D_FOX), row(D_FOX), row(D_FOX), row(D_FOX),
        row(H_FOX), pl.BlockSpec((SUBLANES, tm), lambda i: (0, i)),
        row(D_LRU), row(D_LRU),
        row(D_BAND), row(D_BAND), row(D_BAND), row(D_BAND), row(D_BAND),
    )
    return pl.pallas_call(
        _pre_kernel,
        grid=(nt,),
        in_specs=[row(D_MODEL), full(g), full(wfox), full(wf), full(bf), full(wlru), full(wband)],
        out_specs=out_specs,
        out_shape=out_shape,
        compiler_params=_cparams(("arbitrary",)),
        name="pre",
    )(x, g, wfox, wf, bf, wlru, wband)


def _cum_kernel(x_ref, o_ref):
    y = x_ref[...]
    length = y.shape[1]
    lane = lax.broadcasted_iota(jnp.int32, y.shape, 1)
    s = 1
    while s < length:
        y = y + jnp.where(lane >= s, pltpu.roll(y, s, 1), 0.0)
        s *= 2
    o_ref[...] = y


def _cum_call(x):
    return pl.pallas_call(
        _cum_kernel,
        out_shape=jax.ShapeDtypeStruct(x.shape, F32),
        compiler_params=pltpu.CompilerParams(vmem_limit_bytes=VMEM_LIMIT),
        name="cum",
    )(x)


def _fox_kernel(q_ref, k_ref, v_ref, c_ref, o_ref, *, tb):
    i = pl.program_id(1)
    row = lax.broadcasted_iota(jnp.int32, (tb, tb), 0)
    col = lax.broadcasted_iota(jnp.int32, (tb, tb), 1)
    causal = col <= row
    for h in range(H_FOX):
        hs = slice(h * HEAD_DIM, (h + 1) * HEAD_DIM)
        q = q_ref[:, hs]

        def block(kb, carry, masked):
            m, l, acc = carry
            start = pl.multiple_of(kb * tb, tb)
            kk = k_ref[pl.ds(start, tb), hs]
            vv = v_ref[pl.ds(start, tb), hs]
            s = _dot_nt(q, kk) - c_ref[h:h + 1, pl.ds(start, tb)]
            if masked:
                s = jnp.where(causal, s, NEG_INF)
            m_new = jnp.maximum(m, jnp.max(s, axis=-1, keepdims=True))
            alpha = jnp.exp(m - m_new)
            p = jnp.exp(s - m_new)
            l = alpha * l + jnp.sum(p, axis=-1, keepdims=True)
            acc = alpha * acc + _dot(p.astype(BF16), vv)
            return m_new, l, acc

        init = (jnp.full((tb, 1), NEG_INF, F32), jnp.zeros((tb, 1), F32),
                jnp.zeros((tb, HEAD_DIM), F32))
        carry = lax.fori_loop(0, i, functools.partial(block, masked=False), init)
        m, l, acc = block(i, carry, True)
        o_ref[:, hs] = (acc / l).astype(BF16)


def _fox_call(q, k, v, c, batch, seq, tb):
    nq = seq // tb
    return pl.pallas_call(
        functools.partial(_fox_kernel, tb=tb),
        grid=(batch, nq),
        in_specs=[
            pl.BlockSpec((tb, D_FOX), lambda b, i: (b * nq + i, 0)),
            pl.BlockSpec((seq, D_FOX), lambda b, i: (b, 0)),
            pl.BlockSpec((seq, D_FOX), lambda b, i: (b, 0)),
            pl.BlockSpec((None, SUBLANES, seq), lambda b, i: (b, 0, 0)),
        ],
        out_specs=pl.BlockSpec((tb, D_FOX), lambda b, i: (b * nq + i, 0)),
        out_shape=jax.ShapeDtypeStruct((batch * seq, D_FOX), BF16),
        compiler_params=_cparams(("arbitrary", "arbitrary")),
        name="fox",
    )(q, k, v, c)


def _fox_s_kernel(q_ref, kh_ref, vh_ref, kn_ref, vn_ref, c_ref, o_ref, *, t, past):
    row = lax.broadcasted_iota(jnp.int32, (t, LANES), 0)
    col = lax.broadcasted_iota(jnp.int32, (t, LANES), 1)
    causal = col <= row
    for h in range(H_FOX):
        hs = slice(h * HEAD_DIM, (h + 1) * HEAD_DIM)
        q = q_ref[:, hs]
        s_h = _dot_nt(q, kh_ref[:, hs].astype(BF16)) - c_ref[h:h + 1, :past]
        s_n = _dot_nt(q, kn_ref[:, hs]) - c_ref[h:h + 1, past:]
        s_n = jnp.where(causal, s_n, NEG_INF)
        m = jnp.maximum(jnp.max(s_h, axis=-1, keepdims=True), jnp.max(s_n, axis=-1, keepdims=True))
        p_h = jnp.exp(s_h - m)
        p_n = jnp.exp(s_n - m)
        l = jnp.sum(p_h, axis=-1, keepdims=True) + jnp.sum(p_n, axis=-1, keepdims=True)
        o = _dot(p_h.astype(BF16), vh_ref[:, hs].astype(BF16)) + _dot(p_n.astype(BF16), vn_ref[:, hs])
        o_ref[:, hs] = (o / l).astype(BF16)


def _fox_s_call(q, k_hist, v_hist, k_new, v_new, c, batch, t, past):
    return pl.pallas_call(
        functools.partial(_fox_s_kernel, t=t, past=past),
        grid=(batch,),
        in_specs=[
            pl.BlockSpec((t, D_FOX), lambda b: (b, 0)),
            pl.BlockSpec((None, past, D_FOX), lambda b: (b, 0, 0)),
            pl.BlockSpec((None, past, D_FOX), lambda b: (b, 0, 0)),
            pl.BlockSpec((None, LANES, D_FOX), lambda b: (b, 0, 0)),
            pl.BlockSpec((None, LANES, D_FOX), lambda b: (b, 0, 0)),
            pl.BlockSpec((None, SUBLANES, past + LANES), lambda b: (b, 0, 0)),
        ],
        out_specs=pl.BlockSpec((t, D_FOX), lambda b: (b, 0)),
        out_shape=jax.ShapeDtypeStruct((batch * t, D_FOX), BF16),
        compiler_params=_cparams(("arbitrary",)),
        name="fox_s",
    )(q, k_hist, v_hist, k_new, v_new, c)


def _lru_kernel(xb_ref, yb_ref, hx_ref, h0_ref, cw_ref, cb_ref, wa_ref, ba_ref, wx_ref, bx_ref,
                lam_ref, o_ref, hl_ref, xtail_ref, hcar_ref, *, tm, seg):
    multi = seg < tm
    x = xb_ref[...]
    row = lax.broadcasted_iota(jnp.int32, (tm, D_LRU), 0)
    rmod = (row & (seg - 1)) if multi else row

    if multi:
        prevs = [jnp.where(rmod >= d, pltpu.roll(x, d, 0), hx_ref[LRU_CONV - 1 - d])
                 for d in range(1, LRU_CONV)]
        h0 = h0_ref[...]
    else:
        @pl.when(pl.program_id(1) == 0)
        def _():
            xtail_ref[...] = hx_ref[...]
            hcar_ref[...] = h0_ref[...]

        tail = xtail_ref[...]
        prevs = []
        for d in range(1, LRU_CONV):
            fill = tail[SUBLANES - d:SUBLANES - d + 1]
            for r in range(1, d):
                fill = jnp.where(row == r, tail[SUBLANES - d + r:SUBLANES - d + r + 1], fill)
            prevs.append(jnp.where(row >= d, pltpu.roll(x, d, 0), fill))
        h0 = hcar_ref[SUBLANES - 1:SUBLANES]

    xc = prevs[2] * cw_ref[0:1]
    xc = xc + prevs[1] * cw_ref[1:2]
    xc = xc + prevs[0] * cw_ref[2:3]
    xc = xc + x * cw_ref[3:4]
    xc = xc + cb_ref[...]

    xcb = xc.astype(BF16)
    r = jax.nn.sigmoid(_dot(xcb, wa_ref[...]) + ba_ref[...])
    ig = jax.nn.sigmoid(_dot(xcb, wx_ref[...]) + bx_ref[...])
    log_a = RGLRU_C * r * _log_sigmoid(lam_ref[...])
    a = jnp.exp(log_a)
    th = jnp.tanh(log_a)
    u = jnp.sqrt(-2.0 * th / (1.0 - th)) * (ig * xc)

    s = 1
    while s < seg:
        ok = rmod >= s
        u = jnp.where(ok, a * pltpu.roll(u, s, 0) + u, u)
        a = jnp.where(ok, a * pltpu.roll(a, s, 0), a)
        s *= 2
    hseq = a * h0 + u

    o_ref[...] = (hseq * _gelu(yb_ref[...])).astype(BF16)
    if multi:
        hl_ref[...] = hseq
    else:
        hl_ref[...] = hseq[tm - SUBLANES:]
        xtail_ref[...] = x[tm - SUBLANES:]
        hcar_ref[...] = hseq[tm - SUBLANES:]


def _lru_call(xb, yb, hx, h0, cw, cb, wa, ba, wx, bx, lam, batch, seq, tm):
    n = batch * seq
    multi = seq < tm
    full = lambda a: pl.BlockSpec(a.shape, lambda *_: (0,) * a.ndim)
    if multi:
        grid = (1, 1)
        rows = pl.BlockSpec((tm, D_LRU), lambda b, t: (0, 0))
        hx_spec = full(hx)
        h0_spec = full(h0)
        hl_shape = jax.ShapeDtypeStruct((n, D_LRU), F32)
        hl_spec = rows
    else:
        nt = seq // tm
        grid = (batch, nt)
        rows = pl.BlockSpec((tm, D_LRU), lambda b, t: (b * nt + t, 0))
        hx_spec = pl.BlockSpec((None, SUBLANES, D_LRU), lambda b, t: (b, 0, 0))
        h0_spec = pl.BlockSpec((None, SUBLANES, D_LRU), lambda b, t: (b, 0, 0))
        hl_shape = jax.ShapeDtypeStruct((batch, SUBLANES, D_LRU), F32)
        hl_spec = pl.BlockSpec((None, SUBLANES, D_LRU), lambda b, t: (b, 0, 0))
    return pl.pallas_call(
        functools.partial(_lru_kernel, tm=tm, seg=min(seq, tm)),
        grid=grid,
        in_specs=[rows, rows, hx_spec, h0_spec, full(cw), full(cb), full(wa), full(ba), full(wx),
                  full(bx), full(lam)],
        out_specs=(rows, hl_spec),
        out_shape=(jax.ShapeDtypeStruct((n, D_LRU), BF16), hl_shape),
        scratch_shapes=[pltpu.VMEM((SUBLANES, D_LRU), F32), pltpu.VMEM((SUBLANES, D_LRU), F32)],
        compiler_params=_cparams(("arbitrary", "arbitrary")),
        name="lru",
    )(xb, yb, hx, h0, cw, cb, wa, ba, wx, bx, lam)


def _bias_kernel(tab_ref, o_ref):
    r = lax.broadcasted_iota(jnp.int32, (BAND_TQ, BAND_TK), 0)
    c = lax.broadcasted_iota(jnp.int32, (BAND_TQ, BAND_TK), 1)
    idx = jnp.clip(c - BAND_KEEP - r, -REL_CLIP, CHUNK - 1) + REL_CLIP
    qch = r >> 6
    kch = c >> 6
    inband = (kch >= qch) & (kch <= qch + BAND_PREV)
    for h in range(H_BAND):
        def body(m, acc):
            return jnp.where(idx == m, tab_ref[h, m], acc)
        bias = lax.fori_loop(0, N_REL, body, jnp.zeros((BAND_TQ, BAND_TK), F32))
        o_ref[h] = jnp.where(inband, bias, NEG_INF)


def _bias_call(rel):
    return pl.pallas_call(
        _bias_kernel,
        in_specs=[pl.BlockSpec(memory_space=pltpu.SMEM)],
        out_shape=jax.ShapeDtypeStruct((H_BAND, BAND_TQ, BAND_TK), F32),
        compiler_params=pltpu.CompilerParams(vmem_limit_bytes=VMEM_LIMIT),
        name="band_bias",
    )(rel)


def _band_kernel(q_ref, k_ref, v_ref, bm_ref, o_ref):
    i = pl.program_id(1)
    start = pl.multiple_of(i * BAND_TQ, BAND_TQ)
    col = lax.broadcasted_iota(jnp.int32, (BAND_TQ, BAND_TK), 1)
    exists = (col + i * BAND_TQ) >= BAND_KEEP
    for h in range(H_BAND):
        hs = slice(h * HEAD_DIM, (h + 1) * HEAD_DIM)
        kk = k_ref[pl.ds(start, BAND_TK), hs]
        vv = v_ref[pl.ds(start, BAND_TK), hs]
        s = _dot_nt(q_ref[:, hs], kk) + bm_ref[h]
        s = jnp.where(exists, s, NEG_INF)
        m = jnp.max(s, axis=-1, keepdims=True)
        p = jnp.exp(s - m)
        l = jnp.sum(p, axis=-1, keepdims=True)
        o_ref[:, hs] = (_dot(p.astype(BF16), vv) / l).astype(BF16)


def _band_call(q, kpad, vpad, bm, batch, seq):
    nq = seq // BAND_TQ
    lk = seq + BAND_KEEP
    return pl.pallas_call(
        _band_kernel,
        grid=(batch, nq),
        in_specs=[
            pl.BlockSpec((BAND_TQ, D_BAND), lambda b, i: (b * nq + i, 0)),
            pl.BlockSpec((None, lk, D_BAND), lambda b, i: (b, 0, 0)),
            pl.BlockSpec((None, lk, D_BAND), lambda b, i: (b, 0, 0)),
            pl.BlockSpec(bm.shape, lambda b, i: (0, 0, 0)),
        ],
        out_specs=pl.BlockSpec((BAND_TQ, D_BAND), lambda b, i: (b * nq + i, 0)),
        out_shape=jax.ShapeDtypeStruct((batch * seq, D_BAND), BF16),
        compiler_params=_cparams(("arbitrary", "arbitrary")),
        name="band",
    )(q, kpad, vpad, bm)


def _band_s_kernel(q_ref, kh_ref, vh_ref, kn_ref, vn_ref, bm_ref, o_ref, *, t):
    col = lax.broadcasted_iota(jnp.int32, (t, LANES), 1)
    exists = col < t
    for h in range(H_BAND):
        hs = slice(h * HEAD_DIM, (h + 1) * HEAD_DIM)
        q = q_ref[:, hs]
        s_h = _dot_nt(q, kh_ref[:, hs].astype(BF16)) + bm_ref[h, :t, :BAND_KEEP]
        s_n = _dot_nt(q, kn_ref[:, hs]) + bm_ref[h, :t, BAND_KEEP:]
        s_n = jnp.where(exists, s_n, NEG_INF)
        m = jnp.maximum(jnp.max(s_h, axis=-1, keepdims=True), jnp.max(s_n, axis=-1, keepdims=True))
        p_h = jnp.exp(s_h - m)
        p_n = jnp.exp(s_n - m)
        l = jnp.sum(p_h, axis=-1, keepdims=True) + jnp.sum(p_n, axis=-1, keepdims=True)
        o = _dot(p_h.astype(BF16), vh_ref[:, hs].astype(BF16)) + _dot(p_n.astype(BF16), vn_ref[:, hs])
        o_ref[:, hs] = (o / l).astype(BF16)


def _band_s_call(q, k_hist, v_hist, k_new, v_new, bm, batch, t):
    return pl.pallas_call(
        functools.partial(_band_s_kernel, t=t),
        grid=(batch,),
        in_specs=[
            pl.BlockSpec((t, D_BAND), lambda b: (b, 0)),
            pl.BlockSpec((None, BAND_KEEP, D_BAND), lambda b: (b, 0, 0)),
            pl.BlockSpec((None, BAND_KEEP, D_BAND), lambda b: (b, 0, 0)),
            pl.BlockSpec((None, LANES, D_BAND), lambda b: (b, 0, 0)),
            pl.BlockSpec((None, LANES, D_BAND), lambda b: (b, 0, 0)),
            pl.BlockSpec(bm.shape, lambda b: (0, 0, 0)),
        ],
        out_specs=pl.BlockSpec((t, D_BAND), lambda b: (b, 0)),
        out_shape=jax.ShapeDtypeStruct((batch * t, D_BAND), BF16),
        compiler_params=_cparams(("arbitrary",)),
        name="band_s",
    )(q, k_hist, v_hist, k_new, v_new, bm)


FF_CHUNK = 512


def _post_kernel(x_ref, of_ref, ol_ref, ob_ref, hg_ref, g_ref, wout_ref, wup_ref, fcw_ref, fcb_ref,
                 wdown_ref, y_ref, gt_ref, gcar_ref, *, tm, seg):
    multi = seg < tm
    gb = g_ref[1:2]
    gc = g_ref[2:3]
    gd = g_ref[3:4]
    mix = _dot(of_ref[...], wout_ref[0:D_FOX])
    mix = mix + _dot(ol_ref[...], wout_ref[D_FOX:D_FOX + D_LRU])
    mix = mix + _dot(ob_ref[...], wout_ref[D_FOX + D_LRU:])
    x1 = x_ref[...] + _rms(mix, gb)
    h2 = _rms(x1, gc).astype(BF16)

    row = lax.broadcasted_iota(jnp.int32, (tm, FF_CHUNK), 0)
    rmod = (row & (seg - 1)) if multi else row
    if not multi:
        @pl.when(pl.program_id(1) == 0)
        def _():
            gcar_ref[...] = hg_ref[...]

    acc = jnp.zeros((tm, D_MODEL), F32)
    for c in range(D_FF // FF_CHUNK):
        cs = slice(c * FF_CHUNK, (c + 1) * FF_CHUNK)
        g = _dot(h2, wup_ref[:, cs])
        v = _dot(h2, wup_ref[:, D_FF + c * FF_CHUNK:D_FF + (c + 1) * FF_CHUNK])
        if multi:
            prev1 = jnp.where(rmod >= 1, pltpu.roll(g, 1, 0), hg_ref[1, :, cs])
            prev2 = jnp.where(rmod >= 2, pltpu.roll(g, 2, 0), hg_ref[0, :, cs])
        else:
            t6 = gcar_ref[SUBLANES - 2:SUBLANES - 1, cs]
            t7 = gcar_ref[SUBLANES - 1:SUBLANES, cs]
            prev1 = jnp.where(row >= 1, pltpu.roll(g, 1, 0), t7)
            prev2 = jnp.where(row >= 2, pltpu.roll(g, 2, 0), jnp.where(row == 0, t6, t7))
        gconv = prev2 * fcw_ref[0:1, cs]
        gconv = gconv + prev1 * fcw_ref[1:2, cs]
        gconv = gconv + g * fcw_ref[2:3, cs]
        gconv = gconv + fcb_ref[:, cs]
        act = (_gelu(gconv) * v).astype(BF16)
        acc = acc + _dot(act, wdown_ref[cs, :])
        if multi:
            gt_ref[:, cs] = g
        else:
            gt_ref[:, cs] = g[tm - SUBLANES:]
            gcar_ref[:, cs] = g[tm - SUBLANES:]
    y_ref[...] = x1 + _rms(acc, gd)


def _post_call(x, of, ol, ob, hg, g, wout, wup, fcw, fcb, wdown, batch, seq, tm):
    n = batch * seq
    multi = seq < tm
    full = lambda a: pl.BlockSpec(a.shape, lambda *_: (0,) * a.ndim, pipeline_mode=pl.Buffered(1))
    if multi:
        grid = (1, 1)
        rows = lambda w: pl.BlockSpec((tm, w), lambda b, t: (0, 0))
        hg_spec = pl.BlockSpec(hg.shape, lambda b, t: (0, 0, 0))
        gt_shape = jax.ShapeDtypeStruct((n, D_FF), F32)
        gt_spec = rows(D_FF)
    else:
        nt = seq // tm
        grid = (batch, nt)
        rows = lambda w: pl.BlockSpec((tm, w), lambda b, t: (b * nt + t, 0))
        hg_spec = pl.BlockSpec((None, SUBLANES, D_FF), lambda b, t: (b, 0, 0))
        gt_shape = jax.ShapeDtypeStruct((batch, SUBLANES, D_FF), F32)
        gt_spec = pl.BlockSpec((None, SUBLANES, D_FF), lambda b, t: (b, 0, 0))
    return pl.pallas_call(
        functools.partial(_post_kernel, tm=tm, seg=min(seq, tm)),
        grid=grid,
        in_specs=[rows(D_MODEL), rows(D_FOX), rows(D_LRU), rows(D_BAND), hg_spec, full(g), full(wout),
                  full(wup), full(fcw), full(fcb), full(wdown)],
        out_specs=(rows(D_MODEL), gt_spec),
        out_shape=(jax.ShapeDtypeStruct((n, D_MODEL), F32), gt_shape),
        scratch_shapes=[pltpu.VMEM((SUBLANES, D_FF), F32)],
        compiler_params=_cparams(("arbitrary", "arbitrary")),
        name="post",
    )(x, of, ol, ob, hg, g, wout, wup, fcw, fcb, wdown)


def _block_diag(w):
    nb, bw, _ = w.shape
    eye = jnp.eye(nb, dtype=w.dtype)
    return (eye[:, None, :, None] * w[:, :, None, :]).reshape(nb * bw, nb * bw)


def _prep_layer(norm_g, w_in, b_forget, lru_conv_w, lru_conv_b, lru_wa, lru_ba, lru_wx, lru_bx,
                lru_lambda, rel_bias, w_out, w_up, ffn_conv_w, ffn_conv_b, w_down):
    o = 0
    wfox = w_in[:, o:o + 3 * D_FOX]; o += 3 * D_FOX
    wf = w_in[:, o:o + H_FOX]; o += H_FOX
    wlru = w_in[:, o:o + 2 * D_LRU]; o += 2 * D_LRU
    wband = w_in[:, o:o + 3 * D_BAND]
    return dict(
        g=norm_g,
        wfox=wfox.astype(BF16),
        wf=jnp.pad(wf, ((0, 0), (0, F_PAD - H_FOX))).astype(BF16),
        bf=jnp.pad(b_forget, (0, F_PAD - H_FOX)).reshape(1, F_PAD),
        wlru=wlru.astype(BF16),
        wband=wband.astype(BF16),
        cw=lru_conv_w, cb=lru_conv_b.reshape(1, D_LRU),
        wa=_block_diag(lru_wa).astype(BF16), ba=lru_ba.reshape(1, D_LRU),
        wx=_block_diag(lru_wx).astype(BF16), bx=lru_bx.reshape(1, D_LRU),
        lam=lru_lambda.reshape(1, D_LRU),
        rel=rel_bias,
        wout=w_out.astype(BF16), wup=w_up.astype(BF16),
        fcw=ffn_conv_w, fcb=ffn_conv_b.reshape(1, D_FF),
        wdown=w_down.astype(BF16),
    )


def _prompt_layer(x, p, bm, batch, seq):
    (qf, kf, vf, kf16, vf16, lf, lft, xb, yb, qb, kb, vb, kb16, vb16) = _pre_call(
        x, p["g"][0:1], p["wfox"], p["wf"], p["bf"], p["wlru"], p["wband"], 512)
    c = _cum_call(lft.reshape(SUBLANES * batch, seq))
    c = c.reshape(SUBLANES, batch, seq).transpose(1, 0, 2)
    o_fox = _fox_call(qf, kf16, vf16, c, batch, seq, 512)

    zeros8 = jnp.zeros((batch, SUBLANES, D_LRU), F32)
    o_lru, hl = _lru_call(xb, yb, zeros8, zeros8, p["cw"], p["cb"], p["wa"], p["ba"], p["wx"], p["bx"],
                          p["lam"], batch, seq, 256)

    pad = ((0, 0), (BAND_KEEP, 0), (0, 0))
    kpad = jnp.pad(kb16.reshape(batch, seq, D_BAND), pad)
    vpad = jnp.pad(vb16.reshape(batch, seq, D_BAND), pad)
    o_band = _band_call(qb, kpad, vpad, bm, batch, seq)

    y, gt = _post_call(x, o_fox, o_lru, o_band, jnp.zeros((batch, SUBLANES, D_FF), F32), p["g"], p["wout"],
                       p["wup"], p["fcw"], p["fcb"], p["wdown"], batch, seq, 512)

    keep = min(BAND_KEEP, seq)
    state = (
        kf.reshape(batch, seq, H_FOX, HEAD_DIM),
        vf.reshape(batch, seq, H_FOX, HEAD_DIM),
        lf.reshape(batch, seq, H_FOX),
        kb.reshape(batch, seq, H_BAND, HEAD_DIM)[:, seq - keep:],
        vb.reshape(batch, seq, H_BAND, HEAD_DIM)[:, seq - keep:],
        hl[:, SUBLANES - 1],
        xb.reshape(batch, seq, D_LRU)[:, seq - (LRU_CONV - 1):],
        gt[:, SUBLANES - (FFN_CONV - 1):],
    )
    return y, state


def _sample_layer(x, p, bm, batch, t, fox_k_h, fox_v_h, fox_lf_h, band_k_h, band_v_h, lru_h0, lru_conv_h,
                  ffn_conv_h):
    n = batch * t
    past = fox_k_h.shape[1]
    (qf, kf, vf, kf16, vf16, lf, lft, xb, yb, qb, kb, vb, kb16, vb16) = _pre_call(
        x, p["g"][0:1], p["wfox"], p["wf"], p["bf"], p["wlru"], p["wband"], n)

    lf_hist = jnp.pad(fox_lf_h.transpose(0, 2, 1), ((0, 0), (0, SUBLANES - H_FOX), (0, 0)))
    lf_new = lft.reshape(SUBLANES, batch, t).transpose(1, 0, 2)
    lf_all = jnp.concatenate([lf_hist, jnp.pad(lf_new, ((0, 0), (0, 0), (0, LANES - t)))], axis=2)
    c = _cum_call(lf_all.reshape(batch * SUBLANES, past + LANES)).reshape(batch, SUBLANES, past + LANES)

    rpad = lambda a: jnp.pad(a.reshape(batch, t, a.shape[-1]), ((0, 0), (0, LANES - t), (0, 0)))
    o_fox = _fox_s_call(qf, fox_k_h.reshape(batch, past, D_FOX), fox_v_h.reshape(batch, past, D_FOX),
                        rpad(kf16), rpad(vf16), c, batch, t, past)

    hx = jnp.stack([jnp.pad(lru_conv_h[:, j:], ((0, 0), (0, t - (LRU_CONV - 1 - j)), (0, 0))).reshape(n, D_LRU)
                    for j in range(LRU_CONV - 1)])
    h0 = jnp.repeat(lru_h0, t, axis=0)
    o_lru, hl = _lru_call(xb, yb, hx, h0, p["cw"], p["cb"], p["wa"], p["ba"], p["wx"], p["bx"], p["lam"],
                          batch, t, n)

    o_band = _band_s_call(qb, band_k_h.reshape(batch, BAND_KEEP, D_BAND),
                          band_v_h.reshape(batch, BAND_KEEP, D_BAND), rpad(kb16), rpad(vb16), bm, batch, t)

    hg = jnp.stack([jnp.pad(ffn_conv_h[:, j:], ((0, 0), (0, t - (FFN_CONV - 1 - j)), (0, 0))).reshape(n, D_FF)
                    for j in range(FFN_CONV - 1)])
    y, gt = _post_call(x, o_fox, o_lru, o_band, hg, p["g"], p["wout"], p["wup"], p["fcw"], p["fcb"],
                       p["wdown"], batch, t, n)

    state = (
        kf.reshape(batch, t, H_FOX, HEAD_DIM),
        vf.reshape(batch, t, H_FOX, HEAD_DIM),
        lf.reshape(batch, t, H_FOX),
        kb.reshape(batch, t, H_BAND, HEAD_DIM),
        vb.reshape(batch, t, H_BAND, HEAD_DIM),
        hl.reshape(batch, t, D_LRU)[:, t - 1],
        xb.reshape(batch, t, D_LRU)[:, t - (LRU_CONV - 1):],
        gt.reshape(batch, t, D_FF)[:, t - (FFN_CONV - 1):],
    )
    return y, state


def kernel(x_prompt, x_sample, cache_fox_k, cache_fox_v, cache_fox_logf, cache_band_k, cache_band_v, state_lru_h, state_lru_conv, state_ffn_conv, norm_g, w_in, b_forget, lru_conv_w, lru_conv_b, lru_wa, lru_ba, lru_wx, lru_bx, lru_lambda, rel_bias, w_out, w_up, ffn_conv_w, ffn_conv_b, w_down):
    bp, seq, _ = x_prompt.shape
    bs, t, _ = x_sample.shape
    depth = norm_g.shape[0]
    assert seq % 512 == 0 and t < LANES and (t & (t - 1)) == 0 and cache_band_k.shape[2] == BAND_KEEP
    xp = x_prompt.reshape(bp * seq, D_MODEL)
    xs = x_sample.reshape(bs * t, D_MODEL)
    st_p, st_s = [], []
    for l in range(depth):
        p = _prep_layer(norm_g[l], w_in[l], b_forget[l], lru_conv_w[l], lru_conv_b[l], lru_wa[l], lru_ba[l],
                        lru_wx[l], lru_bx[l], lru_lambda[l], rel_bias[l], w_out[l], w_up[l], ffn_conv_w[l],
                        ffn_conv_b[l], w_down[l])
        bm = _bias_call(p["rel"])
        xp, new_p = _prompt_layer(xp, p, bm, bp, seq)
        xs, new_s = _sample_layer(xs, p, bm, bs, t, cache_fox_k[l], cache_fox_v[l], cache_fox_logf[l],
                                  cache_band_k[l], cache_band_v[l], state_lru_h[l], state_lru_conv[l],
                                  state_ffn_conv[l])
        st_p.append(new_p)
        st_s.append(new_s)

    outs = [xp.reshape(bp, seq, D_MODEL), xs.reshape(bs, t, D_MODEL)]
    for j in range(8):
        outs.append(jnp.stack([st[j] for st in st_p], axis=0))
        outs.append(jnp.stack([st[j] for st in st_s], axis=0))
    return tuple(outs)
```

```python
import functools

import jax
import jax.numpy as jnp
from jax import lax
from jax.experimental import pallas as pl
from jax.experimental.pallas import tpu as pltpu

D_MODEL = 1024
CHUNK = 64
HEAD_DIM = 64
N_HEADS = 4
D_ATT = N_HEADS * HEAD_DIM
D_LRU = 512
LRU_CONV = 4
RGLRU_C = 8.0
BAND_PREV = 8
BAND_KEEP = BAND_PREV * CHUNK
REL_CLIP = 128
N_REL = REL_CLIP + CHUNK
D_FF = 3 * D_MODEL
FFN_CONV = 3
RMS_EPS = 1e-6
NEG_INF = -1e30
ATTN_SCALE = HEAD_DIM ** -0.5

LANES = 128
SUBLANES = 8
HEAD_PAD = 80
KV_ROWS = N_HEADS * HEAD_PAD
Q_COLS = N_HEADS * LANES
N_CPARTS = 3
TILE = 512
LRU_TILE = 256
BAND_TQ = 4 * CHUNK
BAND_TK = BAND_TQ + BAND_KEEP
VMEM_LIMIT = 56 * 1024 * 1024

BF16 = jnp.bfloat16
F32 = jnp.float32


def _cparams(sem):
    return pltpu.CompilerParams(dimension_semantics=sem, vmem_limit_bytes=VMEM_LIMIT)


def _rms(x, g):
    y = x * lax.rsqrt(jnp.mean(x * x, axis=-1, keepdims=True) + RMS_EPS)
    return y * g


def _log_sigmoid(x):
    return jnp.minimum(x, 0.0) - jnp.log1p(jnp.exp(-jnp.abs(x)))


def _gelu(x):
    return jax.nn.gelu(x)


def _dot(a, b):
    return jnp.dot(a, b, preferred_element_type=F32)


def _dot_nt(a, b):
    return lax.dot_general(a, b, (((1,), (1,)), ((), ())), preferred_element_type=F32)


def _cumsum_lanes(x):
    n = x.shape[-1]
    lane = lax.broadcasted_iota(jnp.int32, x.shape, x.ndim - 1)
    s = 1
    while s < n:
        x = x + jnp.where(lane >= s, pltpu.roll(x, s, x.ndim - 1), 0.0)
        s *= 2
    return x


def _rows(vals, n, width):
    r = lax.broadcasted_iota(jnp.int32, (n, width), 0)
    out = jnp.zeros((n, width), F32)
    for j, v in enumerate(vals):
        out = jnp.where(r == j, v, out)
    return out


def _pre_kernel(x_ref, g_ref, wq_ref, wk_ref, wv_ref, wf_ref, bf_ref, wlru_ref, wqb_ref, wkb_ref, wvb_ref,
                qf_ref, kft_ref, vft_ref, kfa_ref, vfa_ref, lft_ref, xb_ref, yb_ref,
                qb_ref, kba_ref, vba_ref, kbt_ref, vbt_ref, ccar_ref, *, tm, nt):
    t = pl.program_id(1)
    helper = HEAD_PAD - HEAD_DIM

    @pl.when(t == 0)
    def _():
        ccar_ref[...] = jnp.zeros_like(ccar_ref)
        flag = _rows([jnp.full((1, tm), NEG_INF, F32)], helper, tm).astype(BF16)
        for h in range(N_HEADS):
            r0 = h * HEAD_PAD
            kba_ref[r0:r0 + HEAD_DIM, :] = jnp.zeros((HEAD_DIM, tm), BF16)
            kba_ref[r0 + HEAD_DIM:r0 + HEAD_PAD, :] = flag
        vba_ref[...] = jnp.zeros_like(vba_ref)

    @pl.when(t > 0)
    def _():
        h = _rms(x_ref[...], g_ref[...]).astype(BF16)
        lane = lax.broadcasted_iota(jnp.int32, (1, Q_COLS), 1) & (LANES - 1)
        ones_f = jnp.where((lane >= HEAD_DIM) & (lane < HEAD_DIM + N_CPARTS), 1.0, 0.0)
        ones_b = jnp.where(lane == HEAD_DIM, 1.0, 0.0)
        qf_ref[...] = (_dot_nt(h, wq_ref[...]) * ATTN_SCALE + ones_f).astype(BF16)
        qb_ref[...] = (_dot_nt(h, wqb_ref[...]) * ATTN_SCALE + ones_b).astype(BF16)

        lf = _log_sigmoid(_dot_nt(wf_ref[...], h) + bf_ref[:, 0:1])
        lft_ref[...] = lf
        c = _cumsum_lanes(lf) + ccar_ref[:, 0:1]
        ccar_ref[...] = jnp.broadcast_to(c[:, tm - 1:tm], ccar_ref.shape)
        c1 = c.astype(BF16)
        r1 = c - c1.astype(F32)
        c2 = r1.astype(BF16)
        c3 = (r1 - c2.astype(F32)).astype(BF16)

        kt = _dot_nt(wk_ref[...], h)
        vt = _dot_nt(wv_ref[...], h)
        kft_ref[...] = kt
        vft_ref[...] = vt
        ones_row = _rows([jnp.ones((1, tm), F32)], helper, tm).astype(BF16)
        for hh in range(N_HEADS):
            r0 = hh * HEAD_PAD
            f0 = hh * HEAD_DIM
            kfa_ref[r0:r0 + HEAD_DIM, :] = kt[f0:f0 + HEAD_DIM].astype(BF16)
            neg_c = [-(p[hh:hh + 1].astype(F32)) for p in (c1, c2, c3)]
            kfa_ref[r0 + HEAD_DIM:r0 + HEAD_PAD, :] = _rows(neg_c, helper, tm).astype(BF16)
            vfa_ref[r0:r0 + HEAD_DIM, :] = vt[f0:f0 + HEAD_DIM].astype(BF16)
            vfa_ref[r0 + HEAD_DIM:r0 + HEAD_PAD, :] = ones_row

        lru = _dot_nt(h, wlru_ref[...])
        xb_ref[...] = lru[:, :D_LRU]
        yb_ref[...] = lru[:, D_LRU:]

        kt = _dot_nt(wkb_ref[...], h)
        vt = _dot_nt(wvb_ref[...], h)
        zero_rows = jnp.zeros((helper, tm), BF16)
        for hh in range(N_HEADS):
            r0 = hh * HEAD_PAD
            f0 = hh * HEAD_DIM
            kba_ref[r0:r0 + HEAD_DIM, :] = kt[f0:f0 + HEAD_DIM].astype(BF16)
            kba_ref[r0 + HEAD_DIM:r0 + HEAD_PAD, :] = zero_rows
            vba_ref[r0:r0 + HEAD_DIM, :] = vt[f0:f0 + HEAD_DIM].astype(BF16)
            vba_ref[r0 + HEAD_DIM:r0 + HEAD_PAD, :] = ones_row

        @pl.when(t == nt)
        def _():
            kbt_ref[...] = kt
            vbt_ref[...] = vt


def _pre_call(x, p, batch, seq):
    tm = TILE
    assert tm == BAND_KEEP
    nt = seq // tm
    n = batch * seq
    tok = lambda w: pl.BlockSpec((tm, w), lambda b, t: (b * nt + jnp.maximum(t - 1, 0), 0))
    feat = lambda r: pl.BlockSpec((None, r, tm), lambda b, t: (b, 0, jnp.maximum(t - 1, 0)))
    padded = pl.BlockSpec((None, KV_ROWS, tm), lambda b, t: (b, 0, t))
    last = pl.BlockSpec((None, D_ATT, tm), lambda b, t: (b, 0, 0))
    full = lambda a: pl.BlockSpec(a.shape, lambda b, t: (0,) * a.ndim)
    sd = jax.ShapeDtypeStruct
    ws = [p["g"][0:1], p["wq"], p["wk"], p["wv"], p["wf"], p["bf"], p["wlru"], p["wqb"], p["wkb"], p["wvb"]]
    out_shape = (
        sd((n, Q_COLS), BF16), sd((batch, D_ATT, seq), F32), sd((batch, D_ATT, seq), F32),
        sd((batch, KV_ROWS, seq), BF16), sd((batch, KV_ROWS, seq), BF16),
        sd((batch, SUBLANES, seq), F32), sd((n, D_LRU), F32), sd((n, D_LRU), F32),
        sd((n, Q_COLS), BF16), sd((batch, KV_ROWS, seq + tm), BF16), sd((batch, KV_ROWS, seq + tm), BF16),
        sd((batch, D_ATT, tm), F32), sd((batch, D_ATT, tm), F32),
    )
    out_specs = (
        tok(Q_COLS), feat(D_ATT), feat(D_ATT), feat(KV_ROWS), feat(KV_ROWS),
        feat(SUBLANES), tok(D_LRU), tok(D_LRU),
        tok(Q_COLS), padded, padded, last, last,
    )
    return pl.pallas_call(
        functools.partial(_pre_kernel, tm=tm, nt=nt),
        grid=(batch, nt + 1),
        in_specs=[tok(D_MODEL)] + [full(w) for w in ws],
        out_specs=out_specs,
        out_shape=out_shape,
        scratch_shapes=[pltpu.VMEM((SUBLANES, LANES), F32)],
        compiler_params=_cparams(("arbitrary", "arbitrary")),
        name="pre",
    )(x, *ws)


def _pre_s_kernel(x_ref, g_ref, wq_ref, wk_ref, wv_ref, wf_ref, bf_ref, wlru_ref, wqb_ref, wkb_ref, wvb_ref,
                  qf_ref, kf_ref, vf_ref, kfn_ref, vfn_ref, lft_ref, xb_ref, yb_ref,
                  qb_ref, kb_ref, vb_ref, kbn_ref, vbn_ref, *, batch, t):
    h = _rms(x_ref[...], g_ref[...]).astype(BF16)
    qf_ref[...] = (_dot_nt(h, wq_ref[...]) * ATTN_SCALE).astype(BF16)
    qb_ref[...] = (_dot_nt(h, wqb_ref[...]) * ATTN_SCALE).astype(BF16)
    lft_ref[...] = _log_sigmoid(_dot_nt(wf_ref[...], h) + bf_ref[:, 0:1])
    lru = _dot_nt(h, wlru_ref[...])
    xb_ref[...] = lru[:, :D_LRU]
    yb_ref[...] = lru[:, D_LRU:]
    for w_ref, o_ref, on_ref in ((wk_ref, kf_ref, kfn_ref), (wv_ref, vf_ref, vfn_ref),
                                 (wkb_ref, kb_ref, kbn_ref), (wvb_ref, vb_ref, vbn_ref)):
        y = _dot_nt(h, w_ref[...])
        o_ref[...] = y
        on_ref[...] = jnp.zeros_like(on_ref)
        yb16 = y.astype(BF16)
        for b in range(batch):
            on_ref[b, 0:t, :] = yb16[b * t:(b + 1) * t]


def _pre_s_call(x, p, batch, t):
    n = batch * t
    sd = jax.ShapeDtypeStruct
    ws = [p["g"][0:1], p["wq"], p["wk"], p["wv"], p["wf"], p["bf"], p["wlru"], p["wqb"], p["wkb"], p["wvb"]]
    new = sd((batch, LANES, D_ATT), BF16)
    tokm = sd((n, D_ATT), F32)
    out_shape = (
        sd((n, Q_COLS), BF16), tokm, tokm, new, new, sd((SUBLANES, n), F32),
        sd((n, D_LRU), F32), sd((n, D_LRU), F32),
        sd((n, Q_COLS), BF16), tokm, tokm, new, new,
    )
    return pl.pallas_call(
        functools.partial(_pre_s_kernel, batch=batch, t=t),
        out_shape=out_shape,
        compiler_params=pltpu.CompilerParams(vmem_limit_bytes=VMEM_LIMIT),
        name="pre_s",
    )(x, *ws)


def _fox_kernel(q_ref, k_ref, v_ref, o_ref, sa_ref, sb_ref, *, tb):
    i = pl.program_id(1)
    row = lax.broadcasted_iota(jnp.int32, (tb, tb), 0)
    col = lax.broadcasted_iota(jnp.int32, (tb, tb), 1)
    causal = col <= row
    heads = range(N_HEADS)
    q = [q_ref[:, h * LANES:h * LANES + HEAD_PAD] for h in heads]

    def scores(kb, s_ref):
        start = pl.multiple_of(kb * tb, tb)
        for h in heads:
            s_ref[h] = _dot(q[h], k_ref[h * HEAD_PAD:(h + 1) * HEAD_PAD, pl.ds(start, tb)])

    def block(kb, s_ref, carry, masked):
        start = pl.multiple_of(kb * tb, tb)
        ps, alphas, ms = [], [], []
        for h in heads:
            sh = s_ref[h]
            if masked:
                sh = jnp.where(causal, sh, NEG_INF)
            m = carry[2 * h]
            m_new = jnp.maximum(m, jnp.max(sh, axis=-1, keepdims=True))
            alphas.append(jnp.exp(m - m_new))
            ps.append(jnp.exp(sh - m_new).astype(BF16))
            ms.append(m_new)
        out = []
        for h in heads:
            pv = _dot_nt(ps[h], v_ref[h * HEAD_PAD:(h + 1) * HEAD_PAD, pl.ds(start, tb)])
            out += [ms[h], alphas[h] * carry[2 * h + 1] + pv]
        return tuple(out)

    init = []
    for h in heads:
        init += [jnp.full((tb, 1), NEG_INF, F32), jnp.zeros((tb, HEAD_PAD), F32)]
    scores(0, sa_ref)

    def pair(j, carry):
        kb = 2 * j
        scores(kb + 1, sb_ref)
        carry = block(kb, sa_ref, carry, False)
        scores(kb + 2, sa_ref)
        return block(kb + 1, sb_ref, carry, False)

    carry = lax.fori_loop(0, i // 2, pair, tuple(init))

    def odd(carry):
        scores(i, sb_ref)
        carry = block(i - 1, sa_ref, carry, False)
        return block(i, sb_ref, carry, True)

    def even(carry):
        return block(i, sa_ref, carry, True)

    carry = lax.cond((i & 1) == 1, odd, even, carry)
    for h in heads:
        acc = carry[2 * h + 1]
        o_ref[:, h * HEAD_DIM:(h + 1) * HEAD_DIM] = (acc[:, :HEAD_DIM] / acc[:, HEAD_DIM:HEAD_DIM + 1]).astype(BF16)


def _fox_call(q, k, v, batch, seq):
    tb = TILE
    nq = seq // tb
    return pl.pallas_call(
        functools.partial(_fox_kernel, tb=tb),
        grid=(batch, nq),
        in_specs=[
            pl.BlockSpec((tb, Q_COLS), lambda b, i: (b * nq + i, 0)),
            pl.BlockSpec((None, KV_ROWS, seq), lambda b, i: (b, 0, 0)),
            pl.BlockSpec((None, KV_ROWS, seq), lambda b, i: (b, 0, 0)),
        ],
        out_specs=pl.BlockSpec((tb, D_ATT), lambda b, i: (b * nq + i, 0)),
        out_shape=jax.ShapeDtypeStruct((batch * seq, D_ATT), BF16),
        scratch_shapes=[pltpu.VMEM((N_HEADS, tb, tb), F32), pltpu.VMEM((N_HEADS, tb, tb), F32)],
        compiler_params=_cparams(("arbitrary", "arbitrary")),
        name="fox",
    )(q, k, v)


def _fox_s_kernel(q_ref, kh_ref, vh_ref, lfh_ref, kn_ref, vn_ref, lfn_ref, o_ref, *, t):
    row = lax.broadcasted_iota(jnp.int32, (t, LANES), 0)
    col = lax.broadcasted_iota(jnp.int32, (t, LANES), 1)
    causal = col <= row
    c_h = _cumsum_lanes(lfh_ref[...])
    past = c_h.shape[-1]
    c_n = c_h[:, past - 1:past] + _cumsum_lanes(lfn_ref[0:N_HEADS, :])
    for h in range(N_HEADS):
        fs = slice(h * HEAD_DIM, (h + 1) * HEAD_DIM)
        q = q_ref[:, h * LANES:h * LANES + HEAD_DIM]
        s_h = _dot(q, kh_ref[fs, :].astype(BF16)) - c_h[h:h + 1]
        s_n = _dot_nt(q, kn_ref[:, fs]) - c_n[h:h + 1]
        s_n = jnp.where(causal, s_n, NEG_INF)
        m = jnp.maximum(jnp.max(s_h, axis=-1, keepdims=True), jnp.max(s_n, axis=-1, keepdims=True))
        p_h = jnp.exp(s_h - m)
        p_n = jnp.exp(s_n - m)
        l = jnp.sum(p_h, axis=-1, keepdims=True) + jnp.sum(p_n, axis=-1, keepdims=True)
        o = _dot_nt(p_h.astype(BF16), vh_ref[fs, :].astype(BF16)) + _dot(p_n.astype(BF16), vn_ref[:, fs])
        o_ref[:, fs] = (o / l).astype(BF16)


def _fox_s_call(q, kt_hist, vt_hist, lf_hist, k_new, v_new, lf_new, layer, batch, t):
    past = kt_hist.shape[-1]
    hist = pl.BlockSpec((None, None, D_ATT, past), lambda b: (layer, b, 0, 0))
    new = pl.BlockSpec((None, LANES, D_ATT), lambda b: (b, 0, 0))
    return pl.pallas_call(
        functools.partial(_fox_s_kernel, t=t),
        grid=(batch,),
        in_specs=[
            pl.BlockSpec((t, Q_COLS), lambda b: (b, 0)),
            hist, hist,
            pl.BlockSpec((None, None, N_HEADS, past), lambda b: (layer, b, 0, 0)),
            new, new,
            pl.BlockSpec((None, SUBLANES, LANES), lambda b: (b, 0, 0)),
        ],
        out_specs=pl.BlockSpec((t, D_ATT), lambda b: (b, 0)),
        out_shape=jax.ShapeDtypeStruct((batch * t, D_ATT), BF16),
        compiler_params=_cparams(("arbitrary",)),
        name="fox_s",
    )(q, kt_hist, vt_hist, lf_hist, k_new, v_new, lf_new)


def _lru_kernel(xb_ref, yb_ref, hx_ref, h0_ref, cw_ref, cb_ref, wa_ref, ba_ref, wx_ref, bx_ref,
                lam_ref, o_ref, hl_ref, xtail_ref, hcar_ref, *, tm, seg):
    multi = seg < tm
    x = xb_ref[...]
    row = lax.broadcasted_iota(jnp.int32, (tm, D_LRU), 0)
    rmod = (row & (seg - 1)) if multi else row

    if multi:
        prevs = [jnp.where(rmod >= d, pltpu.roll(x, d, 0), hx_ref[LRU_CONV - 1 - d])
                 for d in range(1, LRU_CONV)]
        h0 = h0_ref[...]
    else:
        @pl.when(pl.program_id(1) == 0)
        def _():
            xtail_ref[...] = hx_ref[...]
            hcar_ref[...] = h0_ref[...]

        tail = xtail_ref[...]
        prevs = []
        for d in range(1, LRU_CONV):
            fill = tail[SUBLANES - d:SUBLANES - d + 1]
            for r in range(1, d):
                fill = jnp.where(row == r, tail[SUBLANES - d + r:SUBLANES - d + r + 1], fill)
            prevs.append(jnp.where(row >= d, pltpu.roll(x, d, 0), fill))
        h0 = hcar_ref[SUBLANES - 1:SUBLANES]

    xc = prevs[2] * cw_ref[0:1]
    xc = xc + prevs[1] * cw_ref[1:2]
    xc = xc + prevs[0] * cw_ref[2:3]
    xc = xc + x * cw_ref[3:4]
    xc = xc + cb_ref[...]

    xcb = xc.astype(BF16)
    r = jax.nn.sigmoid(_dot(xcb, wa_ref[...]) + ba_ref[...])
    ig = jax.nn.sigmoid(_dot(xcb, wx_ref[...]) + bx_ref[...])
    log_a = RGLRU_C * r * _log_sigmoid(lam_ref[...])
    a = jnp.exp(log_a)
    th = jnp.tanh(log_a)
    u = jnp.sqrt(-2.0 * th / (1.0 - th)) * (ig * xc)

    s = 1
    while s < seg:
        ok = rmod >= s
        u = jnp.where(ok, a * pltpu.roll(u, s, 0) + u, u)
        a = jnp.where(ok, a * pltpu.roll(a, s, 0), a)
        s *= 2
    hseq = a * h0 + u

    o_ref[...] = (hseq * _gelu(yb_ref[...])).astype(BF16)
    if multi:
        hl_ref[...] = hseq
    else:
        hl_ref[...] = hseq[tm - SUBLANES:]
        xtail_ref[...] = x[tm - SUBLANES:]
        hcar_ref[...] = hseq[tm - SUBLANES:]


def _lru_call(xb, yb, hx, h0, p, batch, seq, tm):
    n = batch * seq
    multi = seq < tm
    full = lambda a: pl.BlockSpec(a.shape, lambda *_: (0,) * a.ndim)
    if multi:
        grid = (1, 1)
        rows = pl.BlockSpec((tm, D_LRU), lambda b, t: (0, 0))
        hx_spec = full(hx)
        h0_spec = full(h0)
        hl_shape = jax.ShapeDtypeStruct((n, D_LRU), F32)
        hl_spec = rows
    else:
        nt = seq // tm
        grid = (batch, nt)
        rows = pl.BlockSpec((tm, D_LRU), lambda b, t: (b * nt + t, 0))
        hx_spec = pl.BlockSpec((None, SUBLANES, D_LRU), lambda b, t: (b, 0, 0))
        h0_spec = pl.BlockSpec((None, SUBLANES, D_LRU), lambda b, t: (b, 0, 0))
        hl_shape = jax.ShapeDtypeStruct((batch, SUBLANES, D_LRU), F32)
        hl_spec = pl.BlockSpec((None, SUBLANES, D_LRU), lambda b, t: (b, 0, 0))
    ws = [p["cw"], p["cb"], p["wa"], p["ba"], p["wx"], p["bx"], p["lam"]]
    return pl.pallas_call(
        functools.partial(_lru_kernel, tm=tm, seg=min(seq, tm)),
        grid=grid,
        in_specs=[rows, rows, hx_spec, h0_spec] + [full(w) for w in ws],
        out_specs=(rows, hl_spec),
        out_shape=(jax.ShapeDtypeStruct((n, D_LRU), BF16), hl_shape),
        scratch_shapes=[pltpu.VMEM((SUBLANES, D_LRU), F32), pltpu.VMEM((SUBLANES, D_LRU), F32)],
        compiler_params=_cparams(("arbitrary", "arbitrary")),
        name="lru",
    )(xb, yb, hx, h0, *ws)


def _bias_kernel(u_ref, o_ref):
    r = lax.broadcasted_iota(jnp.int32, (BAND_TQ, BAND_TK), 0)
    c = lax.broadcasted_iota(jnp.int32, (BAND_TQ, BAND_TK), 1)
    dch = (c >> 6) - (r >> 6)
    inband = (dch >= 0) & (dch <= BAND_PREV)
    for h in range(N_HEADS):
        ub = jnp.broadcast_to(u_ref[h:h + 1, :], (BAND_TQ, BAND_TK))
        o_ref[h] = jnp.where(inband, pltpu.roll(ub, 0, 1, stride=1, stride_axis=0), NEG_INF)


def _bias_call(rel):
    left = BAND_KEEP - REL_CLIP
    u = jnp.concatenate([jnp.broadcast_to(rel[:, 0:1], (N_HEADS, left)), rel,
                         jnp.broadcast_to(rel[:, 0:1], (N_HEADS, BAND_TK - left - N_REL))], axis=1)
    return pl.pallas_call(
        _bias_kernel,
        out_shape=jax.ShapeDtypeStruct((N_HEADS, BAND_TQ, BAND_TK), F32),
        compiler_params=pltpu.CompilerParams(vmem_limit_bytes=VMEM_LIMIT),
        name="band_bias",
    )(u)


def _band_kernel(q_ref, k_ref, v_ref, bm_ref, o_ref):
    i = pl.program_id(1)
    start = pl.multiple_of(i * BAND_TQ, BAND_TQ)
    heads = range(N_HEADS)
    s = [_dot(q_ref[:, h * LANES:h * LANES + HEAD_PAD],
              k_ref[h * HEAD_PAD:(h + 1) * HEAD_PAD, pl.ds(start, BAND_TK)]) + bm_ref[h] for h in heads]
    ps = []
    for h in heads:
        m = jnp.max(s[h], axis=-1, keepdims=True)
        ps.append(jnp.exp(s[h] - m).astype(BF16))
    for h in heads:
        ov = _dot_nt(ps[h], v_ref[h * HEAD_PAD:(h + 1) * HEAD_PAD, pl.ds(start, BAND_TK)])
        o_ref[:, h * HEAD_DIM:(h + 1) * HEAD_DIM] = (ov[:, :HEAD_DIM] / ov[:, HEAD_DIM:HEAD_DIM + 1]).astype(BF16)


def _band_call(q, kpad, vpad, bm, batch, seq):
    nq = seq // BAND_TQ
    lk = seq + BAND_KEEP
    return pl.pallas_call(
        _band_kernel,
        grid=(batch, nq),
        in_specs=[
            pl.BlockSpec((BAND_TQ, Q_COLS), lambda b, i: (b * nq + i, 0)),
            pl.BlockSpec((None, KV_ROWS, lk), lambda b, i: (b, 0, 0)),
            pl.BlockSpec((None, KV_ROWS, lk), lambda b, i: (b, 0, 0)),
            pl.BlockSpec(bm.shape, lambda b, i: (0, 0, 0)),
        ],
        out_specs=pl.BlockSpec((BAND_TQ, D_ATT), lambda b, i: (b * nq + i, 0)),
        out_shape=jax.ShapeDtypeStruct((batch * seq, D_ATT), BF16),
        compiler_params=_cparams(("arbitrary", "arbitrary")),
        name="band",
    )(q, kpad, vpad, bm)


def _band_s_kernel(q_ref, kh_ref, vh_ref, kn_ref, vn_ref, bm_ref, o_ref, *, t):
    col = lax.broadcasted_iota(jnp.int32, (t, LANES), 1)
    exists = col < t
    for h in range(N_HEADS):
        fs = slice(h * HEAD_DIM, (h + 1) * HEAD_DIM)
        q = q_ref[:, h * LANES:h * LANES + HEAD_DIM]
        s_h = _dot(q, kh_ref[fs, :].astype(BF16)) + bm_ref[h, :t, :BAND_KEEP]
        s_n = _dot_nt(q, kn_ref[:, fs]) + bm_ref[h, :t, BAND_KEEP:BAND_KEEP + LANES]
        s_n = jnp.where(exists, s_n, NEG_INF)
        m = jnp.maximum(jnp.max(s_h, axis=-1, keepdims=True), jnp.max(s_n, axis=-1, keepdims=True))
        p_h = jnp.exp(s_h - m)
        p_n = jnp.exp(s_n - m)
        l = jnp.sum(p_h, axis=-1, keepdims=True) + jnp.sum(p_n, axis=-1, keepdims=True)
        o = _dot_nt(p_h.astype(BF16), vh_ref[fs, :].astype(BF16)) + _dot(p_n.astype(BF16), vn_ref[:, fs])
        o_ref[:, fs] = (o / l).astype(BF16)


def _band_s_call(q, kt_hist, vt_hist, k_new, v_new, bm, layer, batch, t):
    hist = pl.BlockSpec((None, None, D_ATT, BAND_KEEP), lambda b: (layer, b, 0, 0))
    new = pl.BlockSpec((None, LANES, D_ATT), lambda b: (b, 0, 0))
    return pl.pallas_call(
        functools.partial(_band_s_kernel, t=t),
        grid=(batch,),
        in_specs=[pl.BlockSpec((t, Q_COLS), lambda b: (b, 0)), hist, hist, new, new,
                  pl.BlockSpec(bm.shape, lambda b: (0, 0, 0))],
        out_specs=pl.BlockSpec((t, D_ATT), lambda b: (b, 0)),
        out_shape=jax.ShapeDtypeStruct((batch * t, D_ATT), BF16),
        compiler_params=_cparams(("arbitrary",)),
        name="band_s",
    )(q, kt_hist, vt_hist, k_new, v_new, bm)


FF_CHUNK = 512


def _post_kernel(x_ref, of_ref, ol_ref, ob_ref, hg_ref, g_ref, wout_ref, wup_ref, fcw_ref, fcb_ref,
                 wdown_ref, y_ref, gt_ref, gcar_ref, *, tm, seg):
    multi = seg < tm
    gb = g_ref[1:2]
    gc = g_ref[2:3]
    gd = g_ref[3:4]
    mix = _dot(of_ref[...], wout_ref[0:D_ATT])
    mix = mix + _dot(ol_ref[...], wout_ref[D_ATT:D_ATT + D_LRU])
    mix = mix + _dot(ob_ref[...], wout_ref[D_ATT + D_LRU:])
    x1 = x_ref[...] + _rms(mix, gb)
    h2 = _rms(x1, gc).astype(BF16)

    row = lax.broadcasted_iota(jnp.int32, (tm, FF_CHUNK), 0)
    rmod = (row & (seg - 1)) if multi else row
    if not multi:
        @pl.when(pl.program_id(1) == 0)
        def _():
            gcar_ref[...] = hg_ref[...]

    acc = jnp.zeros((tm, D_MODEL), F32)
    for c in range(D_FF // FF_CHUNK):
        cs = slice(c * FF_CHUNK, (c + 1) * FF_CHUNK)
        g = _dot(h2, wup_ref[:, cs])
        v = _dot(h2, wup_ref[:, D_FF + c * FF_CHUNK:D_FF + (c + 1) * FF_CHUNK])
        if multi:
            prev1 = jnp.where(rmod >= 1, pltpu.roll(g, 1, 0), hg_ref[1, :, cs])
            prev2 = jnp.where(rmod >= 2, pltpu.roll(g, 2, 0), hg_ref[0, :, cs])
        else:
            t6 = gcar_ref[SUBLANES - 2:SUBLANES - 1, cs]
            t7 = gcar_ref[SUBLANES - 1:SUBLANES, cs]
            prev1 = jnp.where(row >= 1, pltpu.roll(g, 1, 0), t7)
            prev2 = jnp.where(row >= 2, pltpu.roll(g, 2, 0), jnp.where(row == 0, t6, t7))
        gconv = prev2 * fcw_ref[0:1, cs]
        gconv = gconv + prev1 * fcw_ref[1:2, cs]
        gconv = gconv + g * fcw_ref[2:3, cs]
        gconv = gconv + fcb_ref[:, cs]
        act = (_gelu(gconv) * v).astype(BF16)
        acc = acc + _dot(act, wdown_ref[cs, :])
        if multi:
            gt_ref[:, cs] = g
        else:
            gt_ref[:, cs] = g[tm - SUBLANES:]
            gcar_ref[:, cs] = g[tm - SUBLANES:]
    y_ref[...] = x1 + _rms(acc, gd)


def _post_call(x, of, ol, ob, hg, p, batch, seq, tm):
    n = batch * seq
    multi = seq < tm
    full = lambda a: pl.BlockSpec(a.shape, lambda *_: (0,) * a.ndim, pipeline_mode=pl.Buffered(1))
    if multi:
        grid = (1, 1)
        rows = lambda w: pl.BlockSpec((tm, w), lambda b, t: (0, 0))
        hg_spec = pl.BlockSpec(hg.shape, lambda b, t: (0, 0, 0))
        gt_shape = jax.ShapeDtypeStruct((n, D_FF), F32)
        gt_spec = rows(D_FF)
    else:
        nt = seq // tm
        grid = (batch, nt)
        rows = lambda w: pl.BlockSpec((tm, w), lambda b, t: (b * nt + t, 0))
        hg_spec = pl.BlockSpec((None, SUBLANES, D_FF), lambda b, t: (b, 0, 0))
        gt_shape = jax.ShapeDtypeStruct((batch, SUBLANES, D_FF), F32)
        gt_spec = pl.BlockSpec((None, SUBLANES, D_FF), lambda b, t: (b, 0, 0))
    ws = [p["g"], p["wout"], p["wup"], p["fcw"], p["fcb"], p["wdown"]]
    return pl.pallas_call(
        functools.partial(_post_kernel, tm=tm, seg=min(seq, tm)),
        grid=grid,
        in_specs=[rows(D_MODEL), rows(D_ATT), rows(D_LRU), rows(D_ATT), hg_spec] + [full(w) for w in ws],
        out_specs=(rows(D_MODEL), gt_spec),
        out_shape=(jax.ShapeDtypeStruct((n, D_MODEL), F32), gt_shape),
        scratch_shapes=[pltpu.VMEM((SUBLANES, D_FF), F32)],
        compiler_params=_cparams(("arbitrary", "arbitrary")),
        name="post",
    )(x, of, ol, ob, hg, *ws)


def _block_diag(w):
    nb, bw, _ = w.shape
    eye = jnp.eye(nb, dtype=w.dtype)
    return (eye[:, None, :, None] * w[:, :, None, :]).reshape(nb * bw, nb * bw)


def _head_tiles(w):
    k = w.shape[1]
    w = w.reshape(N_HEADS, HEAD_DIM, k)
    return jnp.pad(w, ((0, 0), (0, LANES - HEAD_DIM), (0, 0))).reshape(Q_COLS, k)


def _prep_layer(norm_g, w_in_t, b_forget, lru_conv_w, lru_conv_b, lru_wa, lru_ba, lru_wx, lru_bx,
                lru_lambda, rel_bias, w_out, w_up, ffn_conv_w, ffn_conv_b, w_down):
    o = 0
    wq, wk, wv = (w_in_t[o + j * D_ATT:o + (j + 1) * D_ATT] for j in range(3)); o += 3 * D_ATT
    wf = w_in_t[o:o + N_HEADS]; o += N_HEADS
    wlru = w_in_t[o:o + 2 * D_LRU]; o += 2 * D_LRU
    wqb, wkb, wvb = (w_in_t[o + j * D_ATT:o + (j + 1) * D_ATT] for j in range(3))
    return dict(
        g=norm_g,
        wq=_head_tiles(wq), wk=wk, wv=wv,
        wf=jnp.pad(wf, ((0, SUBLANES - N_HEADS), (0, 0))),
        bf=jnp.broadcast_to(jnp.pad(b_forget, (0, SUBLANES - N_HEADS))[:, None], (SUBLANES, LANES)),
        wlru=wlru,
        wqb=_head_tiles(wqb), wkb=wkb, wvb=wvb,
        cw=lru_conv_w, cb=lru_conv_b.reshape(1, D_LRU),
        wa=_block_diag(lru_wa).astype(BF16), ba=lru_ba.reshape(1, D_LRU),
        wx=_block_diag(lru_wx).astype(BF16), bx=lru_bx.reshape(1, D_LRU),
        lam=lru_lambda.reshape(1, D_LRU),
        rel=rel_bias,
        wout=w_out.astype(BF16), wup=w_up.astype(BF16),
        fcw=ffn_conv_w, fcb=ffn_conv_b.reshape(1, D_FF),
        wdown=w_down.astype(BF16),
    )


def _prompt_layer(x, p, bm, batch, seq):
    (qf, kft, vft, kfa, vfa, lft, xb, yb, qb, kba, vba, kbt, vbt) = _pre_call(x, p, batch, seq)
    o_fox = _fox_call(qf, kfa, vfa, batch, seq)
    zeros8 = jnp.zeros((batch, SUBLANES, D_LRU), F32)
    o_lru, hl = _lru_call(xb, yb, zeros8, zeros8, p, batch, seq, LRU_TILE)
    o_band = _band_call(qb, kba, vba, bm, batch, seq)
    y, gt = _post_call(x, o_fox, o_lru, o_band, jnp.zeros((batch, SUBLANES, D_FF), F32), p, batch, seq, TILE)
    state = (kft, vft, lft, kbt, vbt, hl[:, SUBLANES - 1],
             xb.reshape(batch, seq, D_LRU)[:, seq - (LRU_CONV - 1):], gt[:, SUBLANES - (FFN_CONV - 1):])
    return y, state


def _sample_layer(x, p, bm, layer, batch, t, fox_kt, fox_vt, fox_lf, band_kt, band_vt, lru_h0, lru_conv_h,
                  ffn_conv_h):
    n = batch * t
    (qf, kf, vf, kfn, vfn, lft, xb, yb, qb, kb, vb, kbn, vbn) = _pre_s_call(x, p, batch, t)
    lf_new = jnp.pad(lft.reshape(SUBLANES, batch, t).transpose(1, 0, 2), ((0, 0), (0, 0), (0, LANES - t)))
    o_fox = _fox_s_call(qf, fox_kt, fox_vt, fox_lf, kfn, vfn, lf_new, layer, batch, t)

    hx = jnp.stack([jnp.pad(lru_conv_h[:, j:], ((0, 0), (0, t - (LRU_CONV - 1 - j)), (0, 0))).reshape(n, D_LRU)
                    for j in range(LRU_CONV - 1)])
    h0 = jnp.repeat(lru_h0, t, axis=0)
    o_lru, hl = _lru_call(xb, yb, hx, h0, p, batch, t, n)
    o_band = _band_s_call(qb, band_kt, band_vt, kbn, vbn, bm, layer, batch, t)
    hg = jnp.stack([jnp.pad(ffn_conv_h[:, j:], ((0, 0), (0, t - (FFN_CONV - 1 - j)), (0, 0))).reshape(n, D_FF)
                    for j in range(FFN_CONV - 1)])
    y, gt = _post_call(x, o_fox, o_lru, o_band, hg, p, batch, t, n)

    state = (
        kf.reshape(batch, t, N_HEADS, HEAD_DIM),
        vf.reshape(batch, t, N_HEADS, HEAD_DIM),
        lft.reshape(SUBLANES, batch, t)[:N_HEADS].transpose(1, 2, 0),
        kb.reshape(batch, t, N_HEADS, HEAD_DIM),
        vb.reshape(batch, t, N_HEADS, HEAD_DIM),
        hl.reshape(batch, t, D_LRU)[:, t - 1],
        xb.reshape(batch, t, D_LRU)[:, t - (LRU_CONV - 1):],
        gt.reshape(batch, t, D_FF)[:, t - (FFN_CONV - 1):],
    )
    return y, state


def _feature_major(c):
    d, b, s = c.shape[:3]
    return c.transpose(0, 1, 3, 4, 2).reshape(d, b, D_ATT, s)


def _token_major(c):
    d, b, _, s = c.shape
    return c.reshape(d, b, N_HEADS, HEAD_DIM, s).transpose(0, 1, 4, 2, 3)


def kernel(x_prompt, x_sample, cache_fox_k, cache_fox_v, cache_fox_logf, cache_band_k, cache_band_v, state_lru_h, state_lru_conv, state_ffn_conv, norm_g, w_in, b_forget, lru_conv_w, lru_conv_b, lru_wa, lru_ba, lru_wx, lru_bx, lru_lambda, rel_bias, w_out, w_up, ffn_conv_w, ffn_conv_b, w_down):
    bp, seq, _ = x_prompt.shape
    bs, t, _ = x_sample.shape
    depth = norm_g.shape[0]
    assert seq % TILE == 0 and t < LANES and (t & (t - 1)) == 0 and cache_band_k.shape[2] == BAND_KEEP
    assert cache_fox_k.shape[2] % LANES == 0
    xp = x_prompt.reshape(bp * seq, D_MODEL)
    xs = x_sample.reshape(bs * t, D_MODEL)
    w_in_t = w_in.transpose(0, 2, 1).astype(BF16)
    fox_kt, fox_vt = _feature_major(cache_fox_k), _feature_major(cache_fox_v)
    band_kt, band_vt = _feature_major(cache_band_k), _feature_major(cache_band_v)
    fox_lf = cache_fox_logf.transpose(0, 1, 3, 2)
    st_p, st_s = [], []
    for l in range(depth):
        p = _prep_layer(norm_g[l], w_in_t[l], b_forget[l], lru_conv_w[l], lru_conv_b[l], lru_wa[l], lru_ba[l],
                        lru_wx[l], lru_bx[l], lru_lambda[l], rel_bias[l], w_out[l], w_up[l], ffn_conv_w[l],
                        ffn_conv_b[l], w_down[l])
        bm = _bias_call(p["rel"])
        xp, new_p = _prompt_layer(xp, p, bm, bp, seq)
        xs, new_s = _sample_layer(xs, p, bm, l, bs, t, fox_kt, fox_vt, fox_lf, band_kt, band_vt,
                                  state_lru_h[l], state_lru_conv[l], state_ffn_conv[l])
        st_p.append(new_p)
        st_s.append(new_s)

    stack = lambda states, j: jnp.stack([st[j] for st in states], axis=0)
    outs = [xp.reshape(bp, seq, D_MODEL), xs.reshape(bs, t, D_MODEL)]
    for j in range(8):
        sp, ss = stack(st_p, j), stack(st_s, j)
        if j in (0, 1, 3, 4):
            sp = _token_major(sp)
        elif j == 2:
            sp = sp[:, :, :N_HEADS].transpose(0, 1, 3, 2)
        outs += [sp, ss]
    return tuple(outs)
```

```python
import functools

import jax
import jax.numpy as jnp
from jax import lax
from jax.experimental import pallas as pl
from jax.experimental.pallas import tpu as pltpu

D_MODEL = 1024
CHUNK = 64
HEAD_DIM = 64
N_HEADS = 4
D_ATT = N_HEADS * HEAD_DIM
D_LRU = 512
LRU_HALF = D_LRU // 2
LRU_CONV = 4
RGLRU_C = 8.0
BAND_PREV = 8
BAND_KEEP = BAND_PREV * CHUNK
REL_CLIP = 128
N_REL = REL_CLIP + CHUNK
D_FF = 3 * D_MODEL
FF_CHUNK = 512
FFN_CONV = 3
RMS_EPS = 1e-6
NEG_INF = -1e30
ATTN_SCALE = HEAD_DIM ** -0.5

LANES = 128
SUBLANES = 8
HEAD_PAD = 80
KV_ROWS = N_HEADS * HEAD_PAD
Q_COLS = N_HEADS * LANES
N_CPARTS = 3
TILE = 512
BAND_TQ = 4 * CHUNK
BAND_TK = BAND_TQ + BAND_KEEP
VMEM_LIMIT = 56 * 1024 * 1024

BF16 = jnp.bfloat16
F32 = jnp.float32


def _cparams(sem):
    return pltpu.CompilerParams(dimension_semantics=sem, vmem_limit_bytes=VMEM_LIMIT)


def _rms(x, g):
    y = x * lax.rsqrt(jnp.mean(x * x, axis=-1, keepdims=True) + RMS_EPS)
    return y * g


def _log_sigmoid(x):
    return jnp.minimum(x, 0.0) - jnp.log1p(jnp.exp(-jnp.abs(x)))


def _gelu(x):
    return jax.nn.gelu(x)


def _dot(a, b):
    return jnp.dot(a, b, preferred_element_type=F32)


def _dot_nt(a, b):
    return lax.dot_general(a, b, (((1,), (1,)), ((), ())), preferred_element_type=F32)


def _cumsum_lanes(x):
    n = x.shape[-1]
    lane = lax.broadcasted_iota(jnp.int32, x.shape, x.ndim - 1)
    s = 1
    while s < n:
        x = x + jnp.where(lane >= s, pltpu.roll(x, s, x.ndim - 1), 0.0)
        s *= 2
    return x


def _rows(vals, n, width):
    r = lax.broadcasted_iota(jnp.int32, (n, width), 0)
    out = jnp.zeros((n, width), F32)
    for j, v in enumerate(vals):
        out = jnp.where(r == j, v, out)
    return out


def _shift_rows(x3, tail, d):
    r = pltpu.roll(x3, d, 1)
    prev = jnp.concatenate([pltpu.roll(tail[None], d, 1), r[:-1]], axis=0)
    sub = lax.broadcasted_iota(jnp.int32, x3.shape, 1)
    return jnp.where(sub >= d, r, prev)


def _gate_logits(xc, wa_ref, wx_ref):
    xcb = xc.astype(BF16)

    def block_diag(w_ref):
        return jnp.concatenate([_dot(xcb[:, :LRU_HALF], w_ref[0]), _dot(xcb[:, LRU_HALF:], w_ref[1])], axis=1)

    return block_diag(wa_ref), block_diag(wx_ref)


def _gate_terms(logits, xc, ba, bx, lam):
    r = jax.nn.sigmoid(logits[0] + ba)
    ig = jax.nn.sigmoid(logits[1] + bx)
    log_a = RGLRU_C * r * _log_sigmoid(lam)
    a = jnp.exp(log_a)
    th = jnp.tanh(log_a)
    u = jnp.sqrt(-2.0 * th / (1.0 - th)) * (ig * xc)
    return a, u


def _q_tiles(q, helper):
    qr = pltpu.roll(q, HEAD_DIM, 1)
    lane = lax.broadcasted_iota(jnp.int32, (1, LANES), 1)
    tiles = [q[:, 0:LANES], qr[:, LANES:2 * LANES], q[:, LANES:2 * LANES], qr[:, 0:LANES]]
    return jnp.concatenate([jnp.where(lane < HEAD_DIM, t, helper) for t in tiles], axis=1).astype(BF16)


def _pre_kernel(x_ref, g_ref, wq_ref, wk_ref, wv_ref, wf_ref, bf_ref, wlru_ref, wqb_ref, wkb_ref, wvb_ref,
                hx_ref, h0_ref, cw_ref, cb_ref, wa_ref, ba_ref, wx_ref, bx_ref, lam_ref,
                qf_ref, kft_ref, vft_ref, kfa_ref, vfa_ref, lft_ref, ol_ref, hl_ref, xt_ref,
                qb_ref, kba_ref, vba_ref, kbt_ref, vbt_ref, ccar_ref, xtail_ref, hcar_ref, *, tm, nt):
    t = pl.program_id(1)
    helper = HEAD_PAD - HEAD_DIM
    groups = tm // SUBLANES

    @pl.when(t == 0)
    def _():
        ccar_ref[...] = jnp.zeros_like(ccar_ref)
        xtail_ref[...] = hx_ref[...]
        hcar_ref[...] = h0_ref[...]
        flag = _rows([jnp.full((1, tm), NEG_INF, F32)], helper, tm).astype(BF16)
        for h in range(N_HEADS):
            r0 = h * HEAD_PAD
            kba_ref[r0:r0 + HEAD_DIM, :] = jnp.zeros((HEAD_DIM, tm), BF16)
            kba_ref[r0 + HEAD_DIM:r0 + HEAD_PAD, :] = flag
        vba_ref[...] = jnp.zeros_like(vba_ref)

    @pl.when(t > 0)
    def _():
        h = _rms(x_ref[...], g_ref[...]).astype(BF16)
        lru = _dot_nt(h, wlru_ref[...])
        xb = lru[:, :D_LRU]
        yb = lru[:, D_LRU:]

        lane = lax.broadcasted_iota(jnp.int32, (1, LANES), 1)
        ones_f = jnp.where((lane >= HEAD_DIM) & (lane < HEAD_DIM + N_CPARTS), 1.0, 0.0)
        ones_b = jnp.where(lane == HEAD_DIM, 1.0, 0.0)
        qf_ref[...] = _q_tiles(_dot_nt(h, wq_ref[...]) * ATTN_SCALE, ones_f)
        qb_ref[...] = _q_tiles(_dot_nt(h, wqb_ref[...]) * ATTN_SCALE, ones_b)

        x3 = xb.reshape(groups, SUBLANES, D_LRU)
        tail = xtail_ref[...]
        xc = _shift_rows(x3, tail, 3) * cw_ref[0:1]
        xc = xc + _shift_rows(x3, tail, 2) * cw_ref[1:2]
        xc = xc + _shift_rows(x3, tail, 1) * cw_ref[2:3]
        xc = xc + x3 * cw_ref[3:4]
        xc = (xc + cb_ref[...]).reshape(tm, D_LRU)

        lf =_log_sigmoid(_dot_nt(wf_ref[...], h) + bf_ref[:, 0:1])
        lft_ref[...] = lf
        c = _cumsum_lanes(lf) + ccar_ref[:, 0:1]
        ccar_ref[...] = jnp.broadcast_to(c[:, tm - 1:tm], ccar_ref.shape)
        c1 = c.astype(BF16)
        r1 = c - c1.astype(F32)
        c2 = r1.astype(BF16)
        c3 = (r1 - c2.astype(F32)).astype(BF16)
        ones_row = _rows([jnp.ones((1, tm), F32)], helper, tm).astype(BF16)
        zero_rows = jnp.zeros((helper, tm), BF16)

        kt = _dot_nt(wk_ref[...], h)
        kft_ref[...] = kt
        vt = _dot_nt(wv_ref[...], h)
        vft_ref[...] = vt
        for hh in range(N_HEADS):
            r0 = hh * HEAD_PAD
            f0 = hh * HEAD_DIM
            kfa_ref[r0:r0 + HEAD_DIM, :] = kt[f0:f0 + HEAD_DIM].astype(BF16)
            neg_c = [-(p[hh:hh + 1].astype(F32)) for p in (c1, c2, c3)]
            kfa_ref[r0 + HEAD_DIM:r0 + HEAD_PAD, :] = _rows(neg_c, helper, tm).astype(BF16)
            vfa_ref[r0:r0 + HEAD_DIM, :] = vt[f0:f0 + HEAD_DIM].astype(BF16)
            vfa_ref[r0 + HEAD_DIM:r0 + HEAD_PAD, :] = ones_row
        kbt = _dot_nt(wkb_ref[...], h)
        vbt = _dot_nt(wvb_ref[...], h)
        for hh in range(N_HEADS):
            r0 = hh * HEAD_PAD
            f0 = hh * HEAD_DIM
            kba_ref[r0:r0 + HEAD_DIM, :] = kbt[f0:f0 + HEAD_DIM].astype(BF16)
            kba_ref[r0 + HEAD_DIM:r0 + HEAD_PAD, :] = zero_rows
            vba_ref[r0:r0 + HEAD_DIM, :] = vbt[f0:f0 + HEAD_DIM].astype(BF16)
            vba_ref[r0 + HEAD_DIM:r0 + HEAD_PAD, :] = ones_row

        logits = _gate_logits(xc, wa_ref, wx_ref)
        a, u = _gate_terms(logits, xc, ba_ref[...], bx_ref[...], lam_ref[...])
        a = a.reshape(groups, SUBLANES, D_LRU)
        u = u.reshape(groups, SUBLANES, D_LRU)
        sub = lax.broadcasted_iota(jnp.int32, a.shape, 1)
        s = 1
        while s < SUBLANES:
            ok = sub >= s
            u = jnp.where(ok, a * pltpu.roll(u, s, 1) + u, u)
            a = jnp.where(ok, a * pltpu.roll(a, s, 1), a)
            s *= 2
        hprev = hcar_ref[SUBLANES - 1:SUBLANES]
        hs = []
        for gi in range(groups):
            hg = a[gi] * hprev + u[gi]
            hs.append(hg)
            hprev = hg[SUBLANES - 1:SUBLANES]
        hseq = jnp.concatenate(hs, axis=0)
        ol_ref[...] = (hseq * _gelu(yb)).astype(BF16)
        hl_ref[...] = hs[-1]
        hcar_ref[...] = hs[-1]
        xt_ref[...] = xb[tm - SUBLANES:]
        xtail_ref[...] = xb[tm - SUBLANES:]

        @pl.when(t == nt)
        def _():
            kbt_ref[...] = kbt
            vbt_ref[...] = vbt


def _pre_call(x, hx, h0, p, batch, seq):
    tm = TILE
    assert tm == BAND_KEEP
    nt = seq // tm
    n = batch * seq
    tok = lambda w: pl.BlockSpec((tm, w), lambda b, t: (b * nt + jnp.maximum(t - 1, 0), 0))
    feat = lambda r: pl.BlockSpec((None, r, tm), lambda b, t: (b, 0, jnp.maximum(t - 1, 0)))
    padded = pl.BlockSpec((None, KV_ROWS, tm), lambda b, t: (b, 0, t))
    last = pl.BlockSpec((None, D_ATT, tm), lambda b, t: (b, 0, 0))
    state = pl.BlockSpec((None, SUBLANES, D_LRU), lambda b, t: (b, 0, 0))
    full = lambda a: pl.BlockSpec(a.shape, lambda b, t: (0,) * a.ndim)
    sd = jax.ShapeDtypeStruct
    ws1 = [p["g"][0:1], p["wq"], p["wk"], p["wv"], p["wf"], p["bf"], p["wlru"], p["wqb"], p["wkb"], p["wvb"]]
    ws2 = [p["cw"], p["cb"], p["wa"], p["ba"], p["wx"], p["bx"], p["lam"]]
    st = sd((batch, SUBLANES, D_LRU), F32)
    out_shape = (
        sd((n, Q_COLS), BF16), sd((batch, D_ATT, seq), F32), sd((batch, D_ATT, seq), F32),
        sd((batch, KV_ROWS, seq), BF16), sd((batch, KV_ROWS, seq), BF16),
        sd((batch, SUBLANES, seq), F32), sd((n, D_LRU), BF16), st, st,
        sd((n, Q_COLS), BF16), sd((batch, KV_ROWS, seq + tm), BF16), sd((batch, KV_ROWS, seq + tm), BF16),
        sd((batch, D_ATT, tm), F32), sd((batch, D_ATT, tm), F32),
    )
    out_specs = (
        tok(Q_COLS), feat(D_ATT), feat(D_ATT), feat(KV_ROWS), feat(KV_ROWS),
        feat(SUBLANES), tok(D_LRU), state, state,
        tok(Q_COLS), padded, padded, last, last,
    )
    return pl.pallas_call(
        functools.partial(_pre_kernel, tm=tm, nt=nt),
        grid=(batch, nt + 1),
        in_specs=[tok(D_MODEL)] + [full(w) for w in ws1] + [state, state] + [full(w) for w in ws2],
        out_specs=out_specs,
        out_shape=out_shape,
        scratch_shapes=[pltpu.VMEM((SUBLANES, LANES), F32), pltpu.VMEM((SUBLANES, D_LRU), F32),
                        pltpu.VMEM((SUBLANES, D_LRU), F32)],
        compiler_params=_cparams(("arbitrary", "arbitrary")),
        name="pre",
    )(x, *ws1, hx, h0, *ws2)


def _pre_s_kernel(x_ref, g_ref, wq_ref, wk_ref, wv_ref, wf_ref, bf_ref, wlru_ref, wqb_ref, wkb_ref, wvb_ref,
                  qf_ref, kf_ref, vf_ref, kfn_ref, vfn_ref, lft_ref, xb_ref, yb_ref,
                  qb_ref, kb_ref, vb_ref, kbn_ref, vbn_ref, *, batch, t):
    h = _rms(x_ref[...], g_ref[...]).astype(BF16)
    qf_ref[...] = (_dot_nt(h, wq_ref[...]) * ATTN_SCALE).astype(BF16)
    qb_ref[...] = (_dot_nt(h, wqb_ref[...]) * ATTN_SCALE).astype(BF16)
    lft_ref[...] = _log_sigmoid(_dot_nt(wf_ref[...], h) + bf_ref[:, 0:1])
    lru = _dot_nt(h, wlru_ref[...])
    xb_ref[...] = lru[:, :D_LRU]
    yb_ref[...] = lru[:, D_LRU:]
    for w_ref, o_ref, on_ref in ((wk_ref, kf_ref, kfn_ref), (wv_ref, vf_ref, vfn_ref),
                                 (wkb_ref, kb_ref, kbn_ref), (wvb_ref, vb_ref, vbn_ref)):
        y = _dot_nt(h, w_ref[...])
        o_ref[...] = y
        on_ref[...] = jnp.zeros_like(on_ref)
        yb16 = y.astype(BF16)
        for b in range(batch):
            on_ref[b, 0:t, :] = yb16[b * t:(b + 1) * t]


def _pre_s_call(x, p, batch, t):
    n = batch * t
    sd = jax.ShapeDtypeStruct
    ws = [p["g"][0:1], p["wq"], p["wk"], p["wv"], p["wf"], p["bf"], p["wlru"], p["wqb"], p["wkb"], p["wvb"]]
    new = sd((batch, LANES, D_ATT), BF16)
    tokm = sd((n, D_ATT), F32)
    out_shape = (
        sd((n, D_ATT), BF16), tokm, tokm, new, new, sd((SUBLANES, n), F32),
        sd((n, D_LRU), F32), sd((n, D_LRU), F32),
        sd((n, D_ATT), BF16), tokm, tokm, new, new,
    )
    return pl.pallas_call(
        functools.partial(_pre_s_kernel, batch=batch, t=t),
        out_shape=out_shape,
        compiler_params=pltpu.CompilerParams(vmem_limit_bytes=VMEM_LIMIT),
        name="pre_s",
    )(x, *ws)


def _fox_kernel(q_ref, k_ref, v_ref, o_ref, sa_ref, sb_ref, *, tb):
    i = pl.program_id(1)
    row = lax.broadcasted_iota(jnp.int32, (tb, tb), 0)
    col = lax.broadcasted_iota(jnp.int32, (tb, tb), 1)
    causal = col <= row
    heads = range(N_HEADS)
    q = [q_ref[:, h * LANES:h * LANES + HEAD_PAD] for h in heads]

    def scores(kb, s_ref):
        start = pl.multiple_of(kb * tb, tb)
        for h in heads:
            s_ref[h] = _dot(q[h], k_ref[h * HEAD_PAD:(h + 1) * HEAD_PAD, pl.ds(start, tb)])

    def block(kb, s_ref, carry, masked):
        start = pl.multiple_of(kb * tb, tb)
        ps, alphas, ms = [], [], []
        for h in heads:
            sh = s_ref[h]
            if masked:
                sh = jnp.where(causal, sh, NEG_INF)
            m = carry[2 * h]
            m_new = jnp.maximum(m, jnp.max(sh, axis=-1, keepdims=True))
            alphas.append(jnp.exp(m - m_new))
            ps.append(jnp.exp(sh - m_new).astype(BF16))
            ms.append(m_new)
        out = []
        for h in heads:
            pv = _dot_nt(ps[h], v_ref[h * HEAD_PAD:(h + 1) * HEAD_PAD, pl.ds(start, tb)])
            out += [ms[h], alphas[h] * carry[2 * h + 1] + pv]
        return tuple(out)

    init = []
    for h in heads:
        init += [jnp.full((tb, 1), NEG_INF, F32), jnp.zeros((tb, HEAD_PAD), F32)]
    scores(0, sa_ref)

    def pair(j, carry):
        kb = 2 * j
        scores(kb + 1, sb_ref)
        carry = block(kb, sa_ref, carry, False)
        scores(kb + 2, sa_ref)
        return block(kb + 1, sb_ref, carry, False)

    carry = lax.fori_loop(0, i // 2, pair, tuple(init))

    def odd(carry):
        scores(i, sb_ref)
        carry = block(i - 1, sa_ref, carry, False)
        return block(i, sb_ref, carry, True)

    def even(carry):
        return block(i, sa_ref, carry, True)

    carry = lax.cond((i & 1) == 1, odd, even, carry)
    for h in heads:
        acc = carry[2 * h + 1]
        o_ref[:, h * HEAD_DIM:(h + 1) * HEAD_DIM] = (acc[:, :HEAD_DIM] / acc[:, HEAD_DIM:HEAD_DIM + 1]).astype(BF16)


def _fox_call(q, k, v, batch, seq):
    tb = TILE
    nq = seq // tb
    return pl.pallas_call(
        functools.partial(_fox_kernel, tb=tb),
        grid=(batch, nq),
        in_specs=[
            pl.BlockSpec((tb, Q_COLS), lambda b, i: (b * nq + i, 0)),
            pl.BlockSpec((None, KV_ROWS, seq), lambda b, i: (b, 0, 0)),
            pl.BlockSpec((None, KV_ROWS, seq), lambda b, i: (b, 0, 0)),
        ],
        out_specs=pl.BlockSpec((tb, D_ATT), lambda b, i: (b * nq + i, 0)),
        out_shape=jax.ShapeDtypeStruct((batch * seq, D_ATT), BF16),
        scratch_shapes=[pltpu.VMEM((N_HEADS, tb, tb), F32), pltpu.VMEM((N_HEADS, tb, tb), F32)],
        compiler_params=_cparams(("arbitrary", "arbitrary")),
        name="fox",
    )(q, k, v)


def _fox_s_kernel(q_ref, kh_ref, vh_ref, lfh_ref, kn_ref, vn_ref, lfn_ref, o_ref, *, t):
    row = lax.broadcasted_iota(jnp.int32, (t, LANES), 0)
    col = lax.broadcasted_iota(jnp.int32, (t, LANES), 1)
    causal = col <= row
    c_h = _cumsum_lanes(lfh_ref[...])
    past = c_h.shape[-1]
    c_n = c_h[:, past - 1:past] + _cumsum_lanes(lfn_ref[0:N_HEADS, :])
    for h in range(N_HEADS):
        fs = slice(h * HEAD_DIM, (h + 1) * HEAD_DIM)
        q = q_ref[:, fs]
        s_h = _dot(q, kh_ref[fs, :].astype(BF16)) - c_h[h:h + 1]
        s_n = _dot_nt(q, kn_ref[:, fs]) - c_n[h:h + 1]
        s_n = jnp.where(causal, s_n, NEG_INF)
        m = jnp.maximum(jnp.max(s_h, axis=-1, keepdims=True), jnp.max(s_n, axis=-1, keepdims=True))
        p_h = jnp.exp(s_h - m)
        p_n = jnp.exp(s_n - m)
        l = jnp.sum(p_h, axis=-1, keepdims=True) + jnp.sum(p_n, axis=-1, keepdims=True)
        o = _dot_nt(p_h.astype(BF16), vh_ref[fs, :].astype(BF16)) + _dot(p_n.astype(BF16), vn_ref[:, fs])
        o_ref[:, fs] = (o / l).astype(BF16)


def _fox_s_call(q, kt_hist, vt_hist, lf_hist, k_new, v_new, lf_new, layer, batch, t):
    past = kt_hist.shape[-1]
    hist = pl.BlockSpec((None, None, D_ATT, past), lambda b: (layer, b, 0, 0))
    new = pl.BlockSpec((None, LANES, D_ATT), lambda b: (b, 0, 0))
    return pl.pallas_call(
        functools.partial(_fox_s_kernel, t=t),
        grid=(batch,),
        in_specs=[
            pl.BlockSpec((t, D_ATT), lambda b: (b, 0)),
            hist, hist,
            pl.BlockSpec((None, None, N_HEADS, past), lambda b: (layer, b, 0, 0)),
            new, new,
            pl.BlockSpec((None, SUBLANES, LANES), lambda b: (b, 0, 0)),
        ],
        out_specs=pl.BlockSpec((t, D_ATT), lambda b: (b, 0)),
        out_shape=jax.ShapeDtypeStruct((batch * t, D_ATT), BF16),
        compiler_params=_cparams(("arbitrary",)),
        name="fox_s",
    )(q, kt_hist, vt_hist, lf_hist, k_new, v_new, lf_new)


def _lru_s_kernel(xb_ref, yb_ref, hx_ref, h0_ref, cw_ref, cb_ref, wa_ref, ba_ref, wx_ref, bx_ref,
                  lam_ref, o_ref, hl_ref, *, seg):
    x = xb_ref[...]
    rmod = lax.broadcasted_iota(jnp.int32, x.shape, 0) & (seg - 1)
    prevs = [jnp.where(rmod >= d, pltpu.roll(x, d, 0), hx_ref[LRU_CONV - 1 - d]) for d in range(1, LRU_CONV)]
    xc = prevs[2] * cw_ref[0:1]
    xc = xc + prevs[1] * cw_ref[1:2]
    xc = xc + prevs[0] * cw_ref[2:3]
    xc = xc + x * cw_ref[3:4]
    xc = xc + cb_ref[...]
    a, u = _gate_terms(_gate_logits(xc, wa_ref, wx_ref), xc, ba_ref[...], bx_ref[...], lam_ref[...])
    s = 1
    while s < seg:
        ok = rmod >= s
        u = jnp.where(ok, a * pltpu.roll(u, s, 0) + u, u)
        a = jnp.where(ok, a * pltpu.roll(a, s, 0), a)
        s *= 2
    hseq = a * h0_ref[...] + u
    o_ref[...] = (hseq * _gelu(yb_ref[...])).astype(BF16)
    hl_ref[...] = hseq


def _lru_s_call(xb, yb, hx, h0, p, seg):
    n = xb.shape[0]
    return pl.pallas_call(
        functools.partial(_lru_s_kernel, seg=seg),
        out_shape=(jax.ShapeDtypeStruct((n, D_LRU), BF16), jax.ShapeDtypeStruct((n, D_LRU), F32)),
        compiler_params=pltpu.CompilerParams(vmem_limit_bytes=VMEM_LIMIT),
        name="lru_s",
    )(xb, yb, hx, h0, p["cw"], p["cb"], p["wa"], p["ba"], p["wx"], p["bx"], p["lam"])


def _bias_kernel(u_ref, o_ref):
    r = lax.broadcasted_iota(jnp.int32, (BAND_TQ, BAND_TK), 0)
    c = lax.broadcasted_iota(jnp.int32, (BAND_TQ, BAND_TK), 1)
    dch = (c >> 6) - (r >> 6)
    inband = (dch >= 0) & (dch <= BAND_PREV)
    for h in range(N_HEADS):
        ub = jnp.broadcast_to(u_ref[h:h + 1, :], (BAND_TQ, BAND_TK))
        o_ref[h] = jnp.where(inband, pltpu.roll(ub, 0, 1, stride=1, stride_axis=0), NEG_INF)


def _bias_call(rel):
    left = BAND_KEEP - REL_CLIP
    u = jnp.concatenate([jnp.broadcast_to(rel[:, 0:1], (N_HEADS, left)), rel,
                         jnp.broadcast_to(rel[:, 0:1], (N_HEADS, BAND_TK - left - N_REL))], axis=1)
    return pl.pallas_call(
        _bias_kernel,
        out_shape=jax.ShapeDtypeStruct((N_HEADS, BAND_TQ, BAND_TK), F32),
        compiler_params=pltpu.CompilerParams(vmem_limit_bytes=VMEM_LIMIT),
        name="band_bias",
    )(u)


def _band_kernel(q_ref, k_ref, v_ref, bm_ref, o_ref):
    i = pl.program_id(1)
    start = pl.multiple_of(i * BAND_TQ, BAND_TQ)
    heads = range(N_HEADS)
    s = [_dot(q_ref[:, h * LANES:h * LANES + HEAD_PAD],
              k_ref[h * HEAD_PAD:(h + 1) * HEAD_PAD, pl.ds(start, BAND_TK)]) + bm_ref[h] for h in heads]
    ps = []
    for h in heads:
        m = jnp.max(s[h], axis=-1, keepdims=True)
        ps.append(jnp.exp(s[h] - m).astype(BF16))
    for h in heads:
        ov = _dot_nt(ps[h], v_ref[h * HEAD_PAD:(h + 1) * HEAD_PAD, pl.ds(start, BAND_TK)])
        o_ref[:, h * HEAD_DIM:(h + 1) * HEAD_DIM] = (ov[:, :HEAD_DIM] / ov[:, HEAD_DIM:HEAD_DIM + 1]).astype(BF16)


def _band_call(q, kpad, vpad, bm, batch, seq):
    nq = seq // BAND_TQ
    lk = seq + BAND_KEEP
    return pl.pallas_call(
        _band_kernel,
        grid=(batch, nq),
        in_specs=[
            pl.BlockSpec((BAND_TQ, Q_COLS), lambda b, i: (b * nq + i, 0)),
            pl.BlockSpec((None, KV_ROWS, lk), lambda b, i: (b, 0, 0)),
            pl.BlockSpec((None, KV_ROWS, lk), lambda b, i: (b, 0, 0)),
            pl.BlockSpec(bm.shape, lambda b, i: (0, 0, 0)),
        ],
        out_specs=pl.BlockSpec((BAND_TQ, D_ATT), lambda b, i: (b * nq + i, 0)),
        out_shape=jax.ShapeDtypeStruct((batch * seq, D_ATT), BF16),
        compiler_params=_cparams(("arbitrary", "arbitrary")),
        name="band",
    )(q, kpad, vpad, bm)


def _band_s_kernel(q_ref, kh_ref, vh_ref, kn_ref, vn_ref, bm_ref, o_ref, *, t):
    col = lax.broadcasted_iota(jnp.int32, (t, LANES), 1)
    exists = col < t
    for h in range(N_HEADS):
        fs = slice(h * HEAD_DIM, (h + 1) * HEAD_DIM)
        q = q_ref[:, fs]
        s_h = _dot(q, kh_ref[fs, :].astype(BF16)) + bm_ref[h, :t, :BAND_KEEP]
        s_n = _dot_nt(q, kn_ref[:, fs]) + bm_ref[h, :t, BAND_KEEP:BAND_KEEP + LANES]
        s_n = jnp.where(exists, s_n, NEG_INF)
        m = jnp.maximum(jnp.max(s_h, axis=-1, keepdims=True), jnp.max(s_n, axis=-1, keepdims=True))
        p_h = jnp.exp(s_h - m)
        p_n = jnp.exp(s_n - m)
        l = jnp.sum(p_h, axis=-1, keepdims=True) + jnp.sum(p_n, axis=-1, keepdims=True)
        o = _dot_nt(p_h.astype(BF16), vh_ref[fs, :].astype(BF16)) + _dot(p_n.astype(BF16), vn_ref[:, fs])
        o_ref[:, fs] = (o / l).astype(BF16)


def _band_s_call(q, kt_hist, vt_hist, k_new, v_new, bm, layer, batch, t):
    hist = pl.BlockSpec((None, None, D_ATT, BAND_KEEP), lambda b: (layer, b, 0, 0))
    new = pl.BlockSpec((None, LANES, D_ATT), lambda b: (b, 0, 0))
    return pl.pallas_call(
        functools.partial(_band_s_kernel, t=t),
        grid=(batch,),
        in_specs=[pl.BlockSpec((t, D_ATT), lambda b: (b, 0)), hist, hist, new, new,
                  pl.BlockSpec(bm.shape, lambda b: (0, 0, 0))],
        out_specs=pl.BlockSpec((t, D_ATT), lambda b: (b, 0)),
        out_shape=jax.ShapeDtypeStruct((batch * t, D_ATT), BF16),
        compiler_params=_cparams(("arbitrary",)),
        name="band_s",
    )(q, kt_hist, vt_hist, k_new, v_new, bm)


def _post_kernel(x_ref, of_ref, ol_ref, ob_ref, hg_ref, g_ref, wout_ref, wup_ref, fcw_ref, fcb_ref,
                 wdown_ref, y_ref, gt_ref, gcar_ref, *, tm, seg):
    multi = seg < tm
    groups = tm // SUBLANES
    gb = g_ref[1:2]
    gc = g_ref[2:3]
    gd = g_ref[3:4]
    mix = _dot(of_ref[...], wout_ref[0:D_ATT])
    mix = mix + _dot(ol_ref[...], wout_ref[D_ATT:D_ATT + D_LRU])
    mix = mix + _dot(ob_ref[...], wout_ref[D_ATT + D_LRU:])
    x1 = x_ref[...] + _rms(mix, gb)
    h2 = _rms(x1, gc).astype(BF16)

    if multi:
        rmod = lax.broadcasted_iota(jnp.int32, (tm, FF_CHUNK), 0) & (seg - 1)
    else:
        @pl.when(pl.program_id(1) == 0)
        def _():
            gcar_ref[...] = hg_ref[...]

    def up(c):
        return (_dot(h2, wup_ref[:, c * FF_CHUNK:(c + 1) * FF_CHUNK]),
                _dot(h2, wup_ref[:, D_FF + c * FF_CHUNK:D_FF + (c + 1) * FF_CHUNK]))

    n_chunks = D_FF // FF_CHUNK
    acc = jnp.zeros((tm, D_MODEL), F32)
    nxt = up(0)
    for c in range(n_chunks):
        cs = slice(c * FF_CHUNK, (c + 1) * FF_CHUNK)
        g, v = nxt
        if c + 1 < n_chunks:
            nxt = up(c + 1)
        if multi:
            prev1 = jnp.where(rmod >= 1, pltpu.roll(g, 1, 0), hg_ref[1, :, cs])
            prev2 = jnp.where(rmod >= 2, pltpu.roll(g, 2, 0), hg_ref[0, :, cs])
            gt_ref[:, cs] = g
        else:
            g3 = g.reshape(groups, SUBLANES, FF_CHUNK)
            tail = gcar_ref[:, cs]
            prev1 = _shift_rows(g3, tail, 1).reshape(tm, FF_CHUNK)
            prev2 = _shift_rows(g3, tail, 2).reshape(tm, FF_CHUNK)
            gt_ref[:, cs] = g[tm - SUBLANES:]
            gcar_ref[:, cs] = g[tm - SUBLANES:]
        gconv = prev2 * fcw_ref[0:1, cs]
        gconv = gconv + prev1 * fcw_ref[1:2, cs]
        gconv = gconv + g * fcw_ref[2:3, cs]
        gconv = gconv + fcb_ref[:, cs]
        act = (_gelu(gconv) * v).astype(BF16)
        acc = acc + _dot(act, wdown_ref[cs, :])
    y_ref[...] = x1 + _rms(acc, gd)


def _post_call(x, of, ol, ob, hg, p, w, layer, batch, seq, tm):
    n = batch * seq
    multi = seq < tm
    full = lambda a: pl.BlockSpec(a.shape, lambda *_: (0,) * a.ndim, pipeline_mode=pl.Buffered(1))
    layer_of = lambda a: pl.BlockSpec((None,) + a.shape[1:], lambda *_: (layer, 0, 0), pipeline_mode=pl.Buffered(1))
    if multi:
        grid = (1, 1)
        rows = lambda width: pl.BlockSpec((tm, width), lambda b, t: (0, 0))
        hg_spec = pl.BlockSpec(hg.shape, lambda b, t: (0, 0, 0))
        gt_shape = jax.ShapeDtypeStruct((n, D_FF), F32)
        gt_spec = rows(D_FF)
    else:
        nt = seq // tm
        grid = (batch, nt)
        rows = lambda width: pl.BlockSpec((tm, width), lambda b, t: (b * nt + t, 0))
        hg_spec = pl.BlockSpec((None, SUBLANES, D_FF), lambda b, t: (b, 0, 0))
        gt_shape = jax.ShapeDtypeStruct((batch, SUBLANES, D_FF), F32)
        gt_spec = pl.BlockSpec((None, SUBLANES, D_FF), lambda b, t: (b, 0, 0))
    return pl.pallas_call(
        functools.partial(_post_kernel, tm=tm, seg=min(seq, tm)),
        grid=grid,
        in_specs=[rows(D_MODEL), rows(D_ATT), rows(D_LRU), rows(D_ATT), hg_spec, full(p["g"]),
                  layer_of(w["wout"]), layer_of(w["wup"]), full(p["fcw"]), full(p["fcb"]), layer_of(w["wdown"])],
        out_specs=(rows(D_MODEL), gt_spec),
        out_shape=(jax.ShapeDtypeStruct((n, D_MODEL), F32), gt_shape),
        scratch_shapes=[pltpu.VMEM((SUBLANES, D_FF), F32)],
        compiler_params=_cparams(("arbitrary", "arbitrary")),
        name="post",
    )(x, of, ol, ob, hg, p["g"], w["wout"], w["wup"], p["fcw"], p["fcb"], w["wdown"])


def _block_diag_halves(w):
    nb, bw, _ = w.shape
    hb = nb // 2
    eye = jnp.eye(hb, dtype=w.dtype)
    w = w.reshape(2, hb, bw, bw)
    return (eye[None, :, None, :, None] * w[:, :, :, None, :]).reshape(2, hb * bw, hb * bw)


def _prep_layer(norm_g, w_in_t, b_forget, lru_conv_w, lru_conv_b, lru_wa, lru_ba, lru_wx, lru_bx,
                lru_lambda, rel_bias, ffn_conv_w, ffn_conv_b):
    o = 0
    wq, wk, wv = (w_in_t[o + j * D_ATT:o + (j + 1) * D_ATT] for j in range(3)); o += 3 * D_ATT
    wf = w_in_t[o:o + N_HEADS]; o += N_HEADS
    wlru = w_in_t[o:o + 2 * D_LRU]; o += 2 * D_LRU
    wqb, wkb, wvb = (w_in_t[o + j * D_ATT:o + (j + 1) * D_ATT] for j in range(3))
    return dict(
        g=norm_g,
        wq=wq, wk=wk, wv=wv,
        wf=jnp.pad(wf, ((0, SUBLANES - N_HEADS), (0, 0))),
        bf=jnp.broadcast_to(jnp.pad(b_forget, (0, SUBLANES - N_HEADS))[:, None], (SUBLANES, LANES)),
        wlru=wlru,
        wqb=wqb, wkb=wkb, wvb=wvb,
        cw=lru_conv_w, cb=lru_conv_b.reshape(1, D_LRU),
        wa=_block_diag_halves(lru_wa).astype(BF16), ba=lru_ba.reshape(1, D_LRU),
        wx=_block_diag_halves(lru_wx).astype(BF16), bx=lru_bx.reshape(1, D_LRU),
        lam=lru_lambda.reshape(1, D_LRU),
        rel=rel_bias,
        fcw=ffn_conv_w, fcb=ffn_conv_b.reshape(1, D_FF),
    )


def _prompt_layer(x, p, w, bm, layer, batch, seq):
    zeros8 = jnp.zeros((batch, SUBLANES, D_LRU), F32)
    (qf, kft, vft, kfa, vfa, lft, o_lru, hl, xt, qb, kba, vba, kbt, vbt) = _pre_call(x, zeros8, zeros8, p, batch, seq)
    o_fox = _fox_call(qf, kfa, vfa, batch, seq)
    o_band = _band_call(qb, kba, vba, bm, batch, seq)
    y, gt = _post_call(x, o_fox, o_lru, o_band, jnp.zeros((batch, SUBLANES, D_FF), F32), p, w, layer, batch, seq,
                       TILE)
    state = (kft, vft, lft, kbt, vbt, hl[:, SUBLANES - 1], xt[:, SUBLANES - (LRU_CONV - 1):],
             gt[:, SUBLANES - (FFN_CONV - 1):])
    return y, state


def _sample_layer(x, p, w, bm, layer, batch, t, fox_kt, fox_vt, fox_lf, band_kt, band_vt, lru_h0, lru_conv_h,
                  ffn_conv_h):
    n = batch * t
    (qf, kf, vf, kfn, vfn, lft, xb, yb, qb, kb, vb, kbn, vbn) = _pre_s_call(x, p, batch, t)
    lf_new = jnp.pad(lft.reshape(SUBLANES, batch, t).transpose(1, 0, 2), ((0, 0), (0, 0), (0, LANES - t)))
    o_fox = _fox_s_call(qf, fox_kt, fox_vt, fox_lf, kfn, vfn, lf_new, layer, batch, t)

    hx = jnp.stack([jnp.pad(lru_conv_h[:, j:], ((0, 0), (0, t - (LRU_CONV - 1 - j)), (0, 0))).reshape(n, D_LRU)
                    for j in range(LRU_CONV - 1)])
    h0 = jnp.repeat(lru_h0, t, axis=0)
    o_lru, hl = _lru_s_call(xb, yb, hx, h0, p, t)
    o_band = _band_s_call(qb, band_kt, band_vt, kbn, vbn, bm, layer, batch, t)
    hg = jnp.stack([jnp.pad(ffn_conv_h[:, j:], ((0, 0), (0, t - (FFN_CONV - 1 - j)), (0, 0))).reshape(n, D_FF)
                    for j in range(FFN_CONV - 1)])
    y, gt = _post_call(x, o_fox, o_lru, o_band, hg, p, w, layer, batch, t, n)

    state = (
        kf.reshape(batch, t, N_HEADS, HEAD_DIM),
        vf.reshape(batch, t, N_HEADS, HEAD_DIM),
        lft.reshape(SUBLANES, batch, t)[:N_HEADS].transpose(1, 2, 0),
        kb.reshape(batch, t, N_HEADS, HEAD_DIM),
        vb.reshape(batch, t, N_HEADS, HEAD_DIM),
        hl.reshape(batch, t, D_LRU)[:, t - 1],
        xb.reshape(batch, t, D_LRU)[:, t - (LRU_CONV - 1):],
        gt.reshape(batch, t, D_FF)[:, t - (FFN_CONV - 1):],
    )
    return y, state


def _feature_major(c):
    d, b, s = c.shape[:3]
    return c.transpose(0, 1, 3, 4, 2).reshape(d, b, D_ATT, s)


def _token_major(c):
    d, b, _, s = c.shape
    return c.reshape(d, b, N_HEADS, HEAD_DIM, s).transpose(0, 1, 4, 2, 3)


def kernel(x_prompt, x_sample, cache_fox_k, cache_fox_v, cache_fox_logf, cache_band_k, cache_band_v, state_lru_h, state_lru_conv, state_ffn_conv, norm_g, w_in, b_forget, lru_conv_w, lru_conv_b, lru_wa, lru_ba, lru_wx, lru_bx, lru_lambda, rel_bias, w_out, w_up, ffn_conv_w, ffn_conv_b, w_down):
    bp, seq, _ = x_prompt.shape
    bs, t, _ = x_sample.shape
    depth = norm_g.shape[0]
    assert seq % TILE == 0 and t < LANES and (t & (t - 1)) == 0 and cache_band_k.shape[2] == BAND_KEEP
    assert cache_fox_k.shape[2] % LANES == 0
    xp = x_prompt.reshape(bp * seq, D_MODEL)
    xs = x_sample.reshape(bs * t, D_MODEL)
    w_in_t = w_in.transpose(0, 2, 1).astype(BF16)
    w = dict(wout=w_out.astype(BF16), wup=w_up.astype(BF16), wdown=w_down.astype(BF16))
    fox_kt, fox_vt = _feature_major(cache_fox_k), _feature_major(cache_fox_v)
    band_kt, band_vt = _feature_major(cache_band_k), _feature_major(cache_band_v)
    fox_lf = cache_fox_logf.transpose(0, 1, 3, 2)
    st_p, st_s = [], []
    for l in range(depth):
        p = _prep_layer(norm_g[l], w_in_t[l], b_forget[l], lru_conv_w[l], lru_conv_b[l], lru_wa[l], lru_ba[l],
                        lru_wx[l], lru_bx[l], lru_lambda[l], rel_bias[l], ffn_conv_w[l], ffn_conv_b[l])
        bm = _bias_call(p["rel"])
        xp, new_p = _prompt_layer(xp, p, w, bm, l, bp, seq)
        xs, new_s = _sample_layer(xs, p, w, bm, l, bs, t, fox_kt, fox_vt, fox_lf, band_kt, band_vt,
                                  state_lru_h[l], state_lru_conv[l], state_ffn_conv[l])
        st_p.append(new_p)
        st_s.append(new_s)

    stack = lambda states, j: jnp.stack([st[j] for st in states], axis=0)
    outs = [xp.reshape(bp, seq, D_MODEL), xs.reshape(bs, t, D_MODEL)]
    for j in range(8):
        sp, ss = stack(st_p, j), stack(st_s, j)
        if j in (0, 1, 3, 4):
            sp = _token_major(sp)
        elif j == 2:
            sp = sp[:, :, :N_HEADS].transpose(0, 1, 3, 2)
        outs += [sp, ss]
    return tuple(outs)
```

```python
import functools

import jax
import jax.numpy as jnp
from jax import lax
from jax.experimental import pallas as pl
from jax.experimental.pallas import tpu as pltpu

D_MODEL = 1024
CHUNK = 64
HEAD_DIM = 64
N_HEADS = 4
D_ATT = N_HEADS * HEAD_DIM
D_LRU = 512
LRU_HALF = D_LRU // 2
LRU_CONV = 4
RGLRU_C = 8.0
BAND_PREV = 8
BAND_KEEP = BAND_PREV * CHUNK
REL_CLIP = 128
N_REL = REL_CLIP + CHUNK
D_FF = 3 * D_MODEL
FF_CHUNK = 1536
FFN_CONV = 3
RMS_EPS = 1e-6
NEG_INF = -1e30
ATTN_SCALE = HEAD_DIM ** -0.5

LANES = 128
SUBLANES = 8
HEAD_PAD = 80
KV_ROWS = N_HEADS * HEAD_PAD
Q_COLS = N_HEADS * LANES
N_CPARTS = 3
TILE = 512
SAMPLE_GROUP = 4
BAND_TQ = 4 * CHUNK
BAND_TK = BAND_TQ + BAND_KEEP
VMEM_LIMIT = 56 * 1024 * 1024

BF16 = jnp.bfloat16
F32 = jnp.float32


def _cparams(sem):
    return pltpu.CompilerParams(dimension_semantics=sem, vmem_limit_bytes=VMEM_LIMIT)


def _rms(x, g):
    y = x * lax.rsqrt(jnp.mean(x * x, axis=-1, keepdims=True) + RMS_EPS)
    return y * g


def _log_sigmoid(x):
    return jnp.minimum(x, 0.0) - jnp.log1p(jnp.exp(-jnp.abs(x)))


def _gelu(x):
    return jax.nn.gelu(x)


def _dot(a, b):
    return jnp.dot(a, b, preferred_element_type=F32)


def _dot_nt(a, b):
    return lax.dot_general(a, b, (((1,), (1,)), ((), ())), preferred_element_type=F32)


def _cumsum_lanes(x):
    n = x.shape[-1]
    lane = lax.broadcasted_iota(jnp.int32, x.shape, x.ndim - 1)
    s = 1
    while s < n:
        x = x + jnp.where(lane >= s, pltpu.roll(x, s, x.ndim - 1), 0.0)
        s *= 2
    return x


def _rows(vals, n, width):
    r = lax.broadcasted_iota(jnp.int32, (n, width), 0)
    out = jnp.zeros((n, width), F32)
    for j, v in enumerate(vals):
        out = jnp.where(r == j, v, out)
    return out


def _shift_rows(x3, tail, d):
    r = pltpu.roll(x3, d, 1)
    prev = jnp.concatenate([pltpu.roll(tail[None], d, 1), r[:-1]], axis=0)
    sub = lax.broadcasted_iota(jnp.int32, x3.shape, 1)
    return jnp.where(sub >= d, r, prev)


def _gate_logits(xc, wa_ref, wx_ref):
    xcb = xc.astype(BF16)

    def block_diag(w_ref):
        return jnp.concatenate([_dot(xcb[:, :LRU_HALF], w_ref[0]), _dot(xcb[:, LRU_HALF:], w_ref[1])], axis=1)

    return block_diag(wa_ref), block_diag(wx_ref)


def _gate_terms(logits, xc, ba, bx, lam):
    r = jax.nn.sigmoid(logits[0] + ba)
    ig = jax.nn.sigmoid(logits[1] + bx)
    log_a = RGLRU_C * r * _log_sigmoid(lam)
    a = jnp.exp(log_a)
    th = jnp.tanh(log_a)
    u = (jnp.sqrt(-2.0 * th) * lax.rsqrt(1.0 - th)) * (ig * xc)
    return a, u


def _q_tiles(q, helper):
    qr = pltpu.roll(q, HEAD_DIM, 1)
    lane = lax.broadcasted_iota(jnp.int32, (1, LANES), 1)
    tiles = [q[:, 0:LANES], qr[:, LANES:2 * LANES], q[:, LANES:2 * LANES], qr[:, 0:LANES]]
    return jnp.concatenate([jnp.where(lane < HEAD_DIM, t, helper) for t in tiles], axis=1).astype(BF16)


def _pre_kernel(x_ref, g_ref, wq_ref, wkv_ref, bf_ref, wlru_ref, wqb_ref,
                hx_ref, h0_ref, cw_ref, cb_ref, wa_ref, ba_ref, wx_ref, bx_ref, lam_ref,
                qf_ref, kft_ref, vft_ref, kfa_ref, vfa_ref, lft_ref, ol_ref, hl_ref, xt_ref,
                qb_ref, kba_ref, vba_ref, kbt_ref, vbt_ref, ccar_ref, xtail_ref, hcar_ref, *, tm, nt):
    t = pl.program_id(1)
    helper = HEAD_PAD - HEAD_DIM
    groups = tm // SUBLANES

    @pl.when(t == 0)
    def _():
        ccar_ref[...] = jnp.zeros_like(ccar_ref)
        xtail_ref[...] = hx_ref[...]
        hcar_ref[...] = h0_ref[...]
        flag = _rows([jnp.full((1, tm), NEG_INF, F32)], helper, tm).astype(BF16)
        for h in range(N_HEADS):
            r0 = h * HEAD_PAD
            kba_ref[r0:r0 + HEAD_DIM, :] = jnp.zeros((HEAD_DIM, tm), BF16)
            kba_ref[r0 + HEAD_DIM:r0 + HEAD_PAD, :] = flag
        vba_ref[...] = jnp.zeros_like(vba_ref)

    @pl.when(t > 0)
    def _():
        h = _rms(x_ref[...], g_ref[...]).astype(BF16)
        lru = _dot_nt(h, wlru_ref[...])
        xb = lru[:, :D_LRU]
        yb = lru[:, D_LRU:]

        x3 = xb.reshape(groups, SUBLANES, D_LRU)
        tail = xtail_ref[...]
        xc = _shift_rows(x3, tail, 3) * cw_ref[0:1]
        xc = xc + _shift_rows(x3, tail, 2) * cw_ref[1:2]
        xc = xc + _shift_rows(x3, tail, 1) * cw_ref[2:3]
        xc = xc + x3 * cw_ref[3:4]
        xc = (xc + cb_ref[...]).reshape(tm, D_LRU)

        lane = lax.broadcasted_iota(jnp.int32, (1, LANES), 1)
        ones_f = jnp.where((lane >= HEAD_DIM) & (lane < HEAD_DIM + N_CPARTS), 1.0, 0.0)
        ones_b = jnp.where(lane == HEAD_DIM, 1.0, 0.0)
        qf_ref[...] = _q_tiles(_dot_nt(h, wq_ref[...]) * ATTN_SCALE, ones_f)
        qb_ref[...] = _q_tiles(_dot_nt(h, wqb_ref[...]) * ATTN_SCALE, ones_b)

        kv = _dot_nt(wkv_ref[...], h)
        kt, vt, kbt, vbt = (kv[j * D_ATT:(j + 1) * D_ATT] for j in range(4))
        lf = _log_sigmoid(kv[4 * D_ATT:] + bf_ref[:, 0:1])
        lft_ref[...] = lf
        c = _cumsum_lanes(lf) + ccar_ref[:, 0:1]
        ccar_ref[...] = jnp.broadcast_to(c[:, tm - 1:tm], ccar_ref.shape)
        c1 = c.astype(BF16)
        r1 = c - c1.astype(F32)
        c2 = r1.astype(BF16)
        c3 = (r1 - c2.astype(F32)).astype(BF16)
        ones_row = _rows([jnp.ones((1, tm), F32)], helper, tm).astype(BF16)
        zero_rows = jnp.zeros((helper, tm), BF16)

        kft_ref[...] = kt
        vft_ref[...] = vt
        for hh in range(N_HEADS):
            r0 = hh * HEAD_PAD
            f0 = hh * HEAD_DIM
            kfa_ref[r0:r0 + HEAD_DIM, :] = kt[f0:f0 + HEAD_DIM].astype(BF16)
            neg_c = [-(p[hh:hh + 1].astype(F32)) for p in (c1, c2, c3)]
            kfa_ref[r0 + HEAD_DIM:r0 + HEAD_PAD, :] = _rows(neg_c, helper, tm).astype(BF16)
            vfa_ref[r0:r0 + HEAD_DIM, :] = vt[f0:f0 + HEAD_DIM].astype(BF16)
            vfa_ref[r0 + HEAD_DIM:r0 + HEAD_PAD, :] = ones_row
        for hh in range(N_HEADS):
            r0 = hh * HEAD_PAD
            f0 = hh * HEAD_DIM
            kba_ref[r0:r0 + HEAD_DIM, :] = kbt[f0:f0 + HEAD_DIM].astype(BF16)
            kba_ref[r0 + HEAD_DIM:r0 + HEAD_PAD, :] = zero_rows
            vba_ref[r0:r0 + HEAD_DIM, :] = vbt[f0:f0 + HEAD_DIM].astype(BF16)
            vba_ref[r0 + HEAD_DIM:r0 + HEAD_PAD, :] = ones_row

        logits = _gate_logits(xc, wa_ref, wx_ref)
        a, u = _gate_terms(logits, xc, ba_ref[...], bx_ref[...], lam_ref[...])
        a = a.reshape(groups, SUBLANES, D_LRU)
        u = u.reshape(groups, SUBLANES, D_LRU)
        sub = lax.broadcasted_iota(jnp.int32, a.shape, 1)
        s = 1
        while s < SUBLANES:
            ok = sub >= s
            u = jnp.where(ok, a * pltpu.roll(u, s, 1) + u, u)
            a = jnp.where(ok, a * pltpu.roll(a, s, 1), a)
            s *= 2
        hprev = hcar_ref[SUBLANES - 1:SUBLANES]
        hs = []
        for gi in range(groups):
            hg = a[gi] * hprev + u[gi]
            hs.append(hg)
            hprev = hg[SUBLANES - 1:SUBLANES]
        hseq = jnp.concatenate(hs, axis=0)
        ol_ref[...] = (hseq * _gelu(yb)).astype(BF16)
        hl_ref[...] = hs[-1]
        hcar_ref[...] = hs[-1]
        xt_ref[...] = xb[tm - SUBLANES:]
        xtail_ref[...] = xb[tm - SUBLANES:]

        @pl.when(t == nt)
        def _():
            kbt_ref[...] = kbt
            vbt_ref[...] = vbt


def _pre_call(x, hx, h0, p, batch, seq):
    tm = TILE
    assert tm == BAND_KEEP
    nt = seq // tm
    n = batch * seq
    tok = lambda w: pl.BlockSpec((tm, w), lambda b, t: (b * nt + jnp.maximum(t - 1, 0), 0))
    feat = lambda r: pl.BlockSpec((None, r, tm), lambda b, t: (b, 0, jnp.maximum(t - 1, 0)))
    padded = pl.BlockSpec((None, KV_ROWS, tm), lambda b, t: (b, 0, t))
    last = pl.BlockSpec((None, D_ATT, tm), lambda b, t: (b, 0, 0))
    state = pl.BlockSpec((None, SUBLANES, D_LRU), lambda b, t: (b, 0, 0))
    full = lambda a: pl.BlockSpec(a.shape, lambda b, t: (0,) * a.ndim)
    sd = jax.ShapeDtypeStruct
    ws1 = [p["g"][0:1], p["wq"], p["wkv"], p["bf"], p["wlru"], p["wqb"]]
    ws2 = [p["cw"], p["cb"], p["wa"], p["ba"], p["wx"], p["bx"], p["lam"]]
    st = sd((batch, SUBLANES, D_LRU), F32)
    out_shape = (
        sd((n, Q_COLS), BF16), sd((batch, D_ATT, seq), F32), sd((batch, D_ATT, seq), F32),
        sd((batch, KV_ROWS, seq), BF16), sd((batch, KV_ROWS, seq), BF16),
        sd((batch, SUBLANES, seq), F32), sd((n, D_LRU), BF16), st, st,
        sd((n, Q_COLS), BF16), sd((batch, KV_ROWS, seq + tm), BF16), sd((batch, KV_ROWS, seq + tm), BF16),
        sd((batch, D_ATT, tm), F32), sd((batch, D_ATT, tm), F32),
    )
    out_specs = (
        tok(Q_COLS), feat(D_ATT), feat(D_ATT), feat(KV_ROWS), feat(KV_ROWS),
        feat(SUBLANES), tok(D_LRU), state, state,
        tok(Q_COLS), padded, padded, last, last,
    )
    return pl.pallas_call(
        functools.partial(_pre_kernel, tm=tm, nt=nt),
        grid=(batch, nt + 1),
        in_specs=[tok(D_MODEL)] + [full(w) for w in ws1] + [state, state] + [full(w) for w in ws2],
        out_specs=out_specs,
        out_shape=out_shape,
        scratch_shapes=[pltpu.VMEM((SUBLANES, LANES), F32), pltpu.VMEM((SUBLANES, D_LRU), F32),
                        pltpu.VMEM((SUBLANES, D_LRU), F32)],
        compiler_params=_cparams(("arbitrary", "arbitrary")),
        name="pre",
    )(x, *ws1, hx, h0, *ws2)


def _pre_s_kernel(x_ref, g_ref, wq_ref, wk_ref, wv_ref, wf_ref, bf_ref, wlru_ref, wqb_ref, wkb_ref, wvb_ref,
                  qf_ref, kf_ref, vf_ref, kfn_ref, vfn_ref, lft_ref, xb_ref, yb_ref,
                  qb_ref, kb_ref, vb_ref, kbn_ref, vbn_ref, *, batch, t):
    h = _rms(x_ref[...], g_ref[...]).astype(BF16)
    qf_ref[...] = (_dot_nt(h, wq_ref[...]) * ATTN_SCALE).astype(BF16)
    qb_ref[...] = (_dot_nt(h, wqb_ref[...]) * ATTN_SCALE).astype(BF16)
    lft_ref[...] = _log_sigmoid(_dot_nt(wf_ref[...], h) + bf_ref[:, 0:1])
    lru = _dot_nt(h, wlru_ref[...])
    xb_ref[...] = lru[:, :D_LRU]
    yb_ref[...] = lru[:, D_LRU:]
    for w_ref, o_ref, on_ref in ((wk_ref, kf_ref, kfn_ref), (wv_ref, vf_ref, vfn_ref),
                                 (wkb_ref, kb_ref, kbn_ref), (wvb_ref, vb_ref, vbn_ref)):
        y = _dot_nt(h, w_ref[...])
        o_ref[...] = y
        on_ref[...] = jnp.zeros_like(on_ref)
        yb16 = y.astype(BF16)
        for b in range(batch):
            on_ref[b, 0:t, :] = yb16[b * t:(b + 1) * t]


def _pre_s_call(x, p, batch, t):
    n = batch * t
    sd = jax.ShapeDtypeStruct
    ws = [p["g"][0:1], p["wq"], p["wk"], p["wv"], p["wf"], p["bf"], p["wlru"], p["wqb"], p["wkb"], p["wvb"]]
    new = sd((batch, LANES, D_ATT), BF16)
    tokm = sd((n, D_ATT), F32)
    out_shape = (
        sd((n, D_ATT), BF16), tokm, tokm, new, new, sd((SUBLANES, n), F32),
        sd((n, D_LRU), F32), sd((n, D_LRU), F32),
        sd((n, D_ATT), BF16), tokm, tokm, new, new,
    )
    return pl.pallas_call(
        functools.partial(_pre_s_kernel, batch=batch, t=t),
        out_shape=out_shape,
        compiler_params=pltpu.CompilerParams(vmem_limit_bytes=VMEM_LIMIT),
        name="pre_s",
    )(x, *ws)


def _fox_kernel(q_ref, k_ref, v_ref, o_ref, sa_ref, sb_ref, *, tb):
    i = pl.program_id(1)
    row = lax.broadcasted_iota(jnp.int32, (tb, tb), 0)
    col = lax.broadcasted_iota(jnp.int32, (tb, tb), 1)
    causal = col <= row
    heads = range(N_HEADS)
    q = [q_ref[:, h * LANES:h * LANES + HEAD_PAD] for h in heads]

    def scores(kb, s_ref):
        start = pl.multiple_of(kb * tb, tb)
        for h in heads:
            s_ref[h] = _dot(q[h], k_ref[h * HEAD_PAD:(h + 1) * HEAD_PAD, pl.ds(start, tb)])

    def block(kb, s_ref, carry, masked):
        start = pl.multiple_of(kb * tb, tb)
        ps, alphas, ms = [], [], []
        for h in heads:
            sh = s_ref[h]
            if masked:
                sh = jnp.where(causal, sh, NEG_INF)
            m = carry[2 * h]
            m_new = jnp.maximum(m, jnp.max(sh, axis=-1, keepdims=True))
            alphas.append(jnp.exp(m - m_new))
            ps.append(jnp.exp(sh - m_new).astype(BF16))
            ms.append(m_new)
        out = []
        for h in heads:
            pv = _dot_nt(ps[h], v_ref[h * HEAD_PAD:(h + 1) * HEAD_PAD, pl.ds(start, tb)])
            out += [ms[h], alphas[h] * carry[2 * h + 1] + pv]
        return tuple(out)

    init = []
    for h in heads:
        init += [jnp.full((tb, 1), NEG_INF, F32), jnp.zeros((tb, HEAD_PAD), F32)]
    scores(0, sa_ref)

    def pair(j, carry):
        kb = 2 * j
        scores(kb + 1, sb_ref)
        carry = block(kb, sa_ref, carry, False)
        scores(kb + 2, sa_ref)
        return block(kb + 1, sb_ref, carry, False)

    carry = lax.fori_loop(0, i // 2, pair, tuple(init))

    def odd(carry):
        scores(i, sb_ref)
        carry = block(i - 1, sa_ref, carry, False)
        return block(i, sb_ref, carry, True)

    def even(carry):
        return block(i, sa_ref, carry, True)

    carry = lax.cond((i & 1) == 1, odd, even, carry)
    for h in heads:
        acc = carry[2 * h + 1]
        o_ref[:, h * HEAD_DIM:(h + 1) * HEAD_DIM] = (acc[:, :HEAD_DIM] / acc[:, HEAD_DIM:HEAD_DIM + 1]).astype(BF16)


def _fox_call(q, k, v, batch, seq):
    tb = TILE
    nq = seq // tb
    return pl.pallas_call(
        functools.partial(_fox_kernel, tb=tb),
        grid=(batch, nq),
        in_specs=[
            pl.BlockSpec((tb, Q_COLS), lambda b, i: (b * nq + i, 0)),
            pl.BlockSpec((None, KV_ROWS, seq), lambda b, i: (b, 0, 0)),
            pl.BlockSpec((None, KV_ROWS, seq), lambda b, i: (b, 0, 0)),
        ],
        out_specs=pl.BlockSpec((tb, D_ATT), lambda b, i: (b * nq + i, 0)),
        out_shape=jax.ShapeDtypeStruct((batch * seq, D_ATT), BF16),
        scratch_shapes=[pltpu.VMEM((N_HEADS, tb, tb), F32), pltpu.VMEM((N_HEADS, tb, tb), F32)],
        compiler_params=_cparams(("arbitrary", "arbitrary")),
        name="fox",
    )(q, k, v)


def _fox_s_kernel(q_ref, kh_ref, vh_ref, lfh_ref, kn_ref, vn_ref, lfn_ref, o_ref, *, t, gb):
    row = lax.broadcasted_iota(jnp.int32, (t, LANES), 0)
    col = lax.broadcasted_iota(jnp.int32, (t, LANES), 1)
    causal = col <= row
    for g in range(gb):
        rs = slice(g * t, (g + 1) * t)
        c_h = _cumsum_lanes(lfh_ref[g])
        past = c_h.shape[-1]
        c_n = c_h[:, past - 1:past] + _cumsum_lanes(lfn_ref[g, 0:N_HEADS, :])
        for h in range(N_HEADS):
            fs = slice(h * HEAD_DIM, (h + 1) * HEAD_DIM)
            q = q_ref[rs, fs]
            s_h = _dot(q, kh_ref[g, fs, :].astype(BF16)) - c_h[h:h + 1]
            s_n = _dot_nt(q, kn_ref[g, :, fs]) - c_n[h:h + 1]
            s_n = jnp.where(causal, s_n, NEG_INF)
            m = jnp.maximum(jnp.max(s_h, axis=-1, keepdims=True), jnp.max(s_n, axis=-1, keepdims=True))
            p_h = jnp.exp(s_h - m)
            p_n = jnp.exp(s_n - m)
            l = jnp.sum(p_h, axis=-1, keepdims=True) + jnp.sum(p_n, axis=-1, keepdims=True)
            o = (_dot_nt(p_h.astype(BF16), vh_ref[g, fs, :].astype(BF16))
                 + _dot(p_n.astype(BF16), vn_ref[g, :, fs]))
            o_ref[rs, fs] = (o / l).astype(BF16)


def _fox_s_call(q, kt_hist, vt_hist, lf_hist, k_new, v_new, lf_new, layer, batch, t):
    past = kt_hist.shape[-1]
    gb = SAMPLE_GROUP
    hist = pl.BlockSpec((None, gb, D_ATT, past), lambda j: (layer, j, 0, 0))
    new = pl.BlockSpec((gb, LANES, D_ATT), lambda j: (j, 0, 0))
    return pl.pallas_call(
        functools.partial(_fox_s_kernel, t=t, gb=gb),
        grid=(batch // gb,),
        in_specs=[
            pl.BlockSpec((gb * t, D_ATT), lambda j: (j, 0)),
            hist, hist,
            pl.BlockSpec((None, gb, N_HEADS, past), lambda j: (layer, j, 0, 0)),
            new, new,
            pl.BlockSpec((gb, SUBLANES, LANES), lambda j: (j, 0, 0)),
        ],
        out_specs=pl.BlockSpec((gb * t, D_ATT), lambda j: (j, 0)),
        out_shape=jax.ShapeDtypeStruct((batch * t, D_ATT), BF16),
        compiler_params=_cparams(("arbitrary",)),
        name="fox_s",
    )(q, kt_hist, vt_hist, lf_hist, k_new, v_new, lf_new)


def _lru_s_kernel(xb_ref, yb_ref, hx_ref, h0_ref, cw_ref, cb_ref, wa_ref, ba_ref, wx_ref, bx_ref,
                  lam_ref, o_ref, hl_ref, *, seg):
    x = xb_ref[...]
    rmod = lax.broadcasted_iota(jnp.int32, x.shape, 0) & (seg - 1)
    prevs = [jnp.where(rmod >= d, pltpu.roll(x, d, 0), hx_ref[LRU_CONV - 1 - d]) for d in range(1, LRU_CONV)]
    xc = prevs[2] * cw_ref[0:1]
    xc = xc + prevs[1] * cw_ref[1:2]
    xc = xc + prevs[0] * cw_ref[2:3]
    xc = xc + x * cw_ref[3:4]
    xc = xc + cb_ref[...]
    a, u = _gate_terms(_gate_logits(xc, wa_ref, wx_ref), xc, ba_ref[...], bx_ref[...], lam_ref[...])
    s = 1
    while s < seg:
        ok = rmod >= s
        u = jnp.where(ok, a * pltpu.roll(u, s, 0) + u, u)
        a = jnp.where(ok, a * pltpu.roll(a, s, 0), a)
        s *= 2
    hseq = a * h0_ref[...] + u
    o_ref[...] = (hseq * _gelu(yb_ref[...])).astype(BF16)
    hl_ref[...] = hseq


def _lru_s_call(xb, yb, hx, h0, p, seg):
    n = xb.shape[0]
    return pl.pallas_call(
        functools.partial(_lru_s_kernel, seg=seg),
        out_shape=(jax.ShapeDtypeStruct((n, D_LRU), BF16), jax.ShapeDtypeStruct((n, D_LRU), F32)),
        compiler_params=pltpu.CompilerParams(vmem_limit_bytes=VMEM_LIMIT),
        name="lru_s",
    )(xb, yb, hx, h0, p["cw"], p["cb"], p["wa"], p["ba"], p["wx"], p["bx"], p["lam"])


def _bias_kernel(u_ref, o_ref):
    r = lax.broadcasted_iota(jnp.int32, (BAND_TQ, BAND_TK), 0)
    c = lax.broadcasted_iota(jnp.int32, (BAND_TQ, BAND_TK), 1)
    dch = (c >> 6) - (r >> 6)
    inband = (dch >= 0) & (dch <= BAND_PREV)
    for h in range(N_HEADS):
        ub = jnp.broadcast_to(u_ref[h:h + 1, :], (BAND_TQ, BAND_TK))
        o_ref[h] = jnp.where(inband, pltpu.roll(ub, 0, 1, stride=1, stride_axis=0), NEG_INF)


def _bias_call(rel):
    left = BAND_KEEP - REL_CLIP
    u = jnp.concatenate([jnp.broadcast_to(rel[:, 0:1], (N_HEADS, left)), rel,
                         jnp.broadcast_to(rel[:, 0:1], (N_HEADS, BAND_TK - left - N_REL))], axis=1)
    return pl.pallas_call(
        _bias_kernel,
        out_shape=jax.ShapeDtypeStruct((N_HEADS, BAND_TQ, BAND_TK), F32),
        compiler_params=pltpu.CompilerParams(vmem_limit_bytes=VMEM_LIMIT),
        name="band_bias",
    )(u)


def _band_kernel(q_ref, k_ref, v_ref, bm_ref, o_ref):
    i = pl.program_id(1)
    start = pl.multiple_of(i * BAND_TQ, BAND_TQ)
    heads = range(N_HEADS)
    s = [_dot(q_ref[:, h * LANES:h * LANES + HEAD_PAD],
              k_ref[h * HEAD_PAD:(h + 1) * HEAD_PAD, pl.ds(start, BAND_TK)]) + bm_ref[h] for h in heads]
    ps = []
    for h in heads:
        m = jnp.max(s[h], axis=-1, keepdims=True)
        ps.append(jnp.exp(s[h] - m).astype(BF16))
    for h in heads:
        ov = _dot_nt(ps[h], v_ref[h * HEAD_PAD:(h + 1) * HEAD_PAD, pl.ds(start, BAND_TK)])
        o_ref[:, h * HEAD_DIM:(h + 1) * HEAD_DIM] = (ov[:, :HEAD_DIM] / ov[:, HEAD_DIM:HEAD_DIM + 1]).astype(BF16)


def _band_call(q, kpad, vpad, bm, batch, seq):
    nq = seq // BAND_TQ
    lk = seq + BAND_KEEP
    return pl.pallas_call(
        _band_kernel,
        grid=(batch, nq),
        in_specs=[
            pl.BlockSpec((BAND_TQ, Q_COLS), lambda b, i: (b * nq + i, 0)),
            pl.BlockSpec((None, KV_ROWS, lk), lambda b, i: (b, 0, 0)),
            pl.BlockSpec((None, KV_ROWS, lk), lambda b, i: (b, 0, 0)),
            pl.BlockSpec(bm.shape, lambda b, i: (0, 0, 0)),
        ],
        out_specs=pl.BlockSpec((BAND_TQ, D_ATT), lambda b, i: (b * nq + i, 0)),
        out_shape=jax.ShapeDtypeStruct((batch * seq, D_ATT), BF16),
        compiler_params=_cparams(("arbitrary", "arbitrary")),
        name="band",
    )(q, kpad, vpad, bm)


def _band_s_kernel(q_ref, kh_ref, vh_ref, kn_ref, vn_ref, bm_ref, o_ref, *, t, gb):
    col = lax.broadcasted_iota(jnp.int32, (t, LANES), 1)
    exists = col < t
    for g in range(gb):
        rs = slice(g * t, (g + 1) * t)
        for h in range(N_HEADS):
            fs = slice(h * HEAD_DIM, (h + 1) * HEAD_DIM)
            q = q_ref[rs, fs]
            s_h = _dot(q, kh_ref[g, fs, :].astype(BF16)) + bm_ref[h, :t, :BAND_KEEP]
            s_n = _dot_nt(q, kn_ref[g, :, fs]) + bm_ref[h, :t, BAND_KEEP:BAND_KEEP + LANES]
            s_n = jnp.where(exists, s_n, NEG_INF)
            m = jnp.maximum(jnp.max(s_h, axis=-1, keepdims=True), jnp.max(s_n, axis=-1, keepdims=True))
            p_h = jnp.exp(s_h - m)
            p_n = jnp.exp(s_n - m)
            l = jnp.sum(p_h, axis=-1, keepdims=True) + jnp.sum(p_n, axis=-1, keepdims=True)
            o = (_dot_nt(p_h.astype(BF16), vh_ref[g, fs, :].astype(BF16))
                 + _dot(p_n.astype(BF16), vn_ref[g, :, fs]))
            o_ref[rs, fs] = (o / l).astype(BF16)


def _band_s_call(q, kt_hist, vt_hist, k_new, v_new, bm, layer, batch, t):
    gb = SAMPLE_GROUP
    hist = pl.BlockSpec((None, gb, D_ATT, BAND_KEEP), lambda j: (layer, j, 0, 0))
    new = pl.BlockSpec((gb, LANES, D_ATT), lambda j: (j, 0, 0))
    return pl.pallas_call(
        functools.partial(_band_s_kernel, t=t, gb=gb),
        grid=(batch // gb,),
        in_specs=[pl.BlockSpec((gb * t, D_ATT), lambda j: (j, 0)), hist, hist, new, new,
                  pl.BlockSpec(bm.shape, lambda j: (0, 0, 0))],
        out_specs=pl.BlockSpec((gb * t, D_ATT), lambda j: (j, 0)),
        out_shape=jax.ShapeDtypeStruct((batch * t, D_ATT), BF16),
        compiler_params=_cparams(("arbitrary",)),
        name="band_s",
    )(q, kt_hist, vt_hist, k_new, v_new, bm)


def _post_kernel(x_ref, of_ref, ol_ref, ob_ref, hg_ref, g_ref, wout_ref, wup_ref, fcw_ref, fcb_ref,
                 wdown_ref, y_ref, gt_ref, gcar_ref, *, tm, seg):
    multi = seg < tm
    groups = tm // SUBLANES
    gb = g_ref[1:2]
    gc = g_ref[2:3]
    gd = g_ref[3:4]
    mix = _dot(of_ref[...], wout_ref[0:D_ATT])
    mix = mix + _dot(ol_ref[...], wout_ref[D_ATT:D_ATT + D_LRU])
    mix = mix + _dot(ob_ref[...], wout_ref[D_ATT + D_LRU:])
    x1 = x_ref[...] + _rms(mix, gb)
    h2 = _rms(x1, gc).astype(BF16)

    if multi:
        rmod = lax.broadcasted_iota(jnp.int32, (tm, FF_CHUNK), 0) & (seg - 1)
    else:
        @pl.when(pl.program_id(1) == 0)
        def _():
            gcar_ref[...] = hg_ref[...]

    def up(c):
        return (_dot(h2, wup_ref[:, c * FF_CHUNK:(c + 1) * FF_CHUNK]),
                _dot(h2, wup_ref[:, D_FF + c * FF_CHUNK:D_FF + (c + 1) * FF_CHUNK]))

    n_chunks = D_FF // FF_CHUNK
    acc = jnp.zeros((tm, D_MODEL), F32)
    nxt = up(0)
    for c in range(n_chunks):
        cs = slice(c * FF_CHUNK, (c + 1) * FF_CHUNK)
        g, v = nxt
        if c + 1 < n_chunks:
            nxt = up(c + 1)
        if multi:
            prev1 = jnp.where(rmod >= 1, pltpu.roll(g, 1, 0), hg_ref[1, :, cs])
            prev2 = jnp.where(rmod >= 2, pltpu.roll(g, 2, 0), hg_ref[0, :, cs])
            gt_ref[:, cs] = g
        else:
            g3 = g.reshape(groups, SUBLANES, FF_CHUNK)
            tail = gcar_ref[:, cs]
            prev1 = _shift_rows(g3, tail, 1).reshape(tm, FF_CHUNK)
            prev2 = _shift_rows(g3, tail, 2).reshape(tm, FF_CHUNK)
            gt_ref[:, cs] = g[tm - SUBLANES:]
            gcar_ref[:, cs] = g[tm - SUBLANES:]
        gconv = prev2 * fcw_ref[0:1, cs]
        gconv = gconv + prev1 * fcw_ref[1:2, cs]
        gconv = gconv + g * fcw_ref[2:3, cs]
        gconv = gconv + fcb_ref[:, cs]
        act = (_gelu(gconv) * v).astype(BF16)
        acc = acc + _dot(act, wdown_ref[cs, :])
    y_ref[...] = x1 + _rms(acc, gd)


def _post_call(x, of, ol, ob, hg, p, w, layer, batch, seq, tm):
    n = batch * seq
    multi = seq < tm
    full = lambda a: pl.BlockSpec(a.shape, lambda *_: (0,) * a.ndim, pipeline_mode=pl.Buffered(1))
    layer_of = lambda a: pl.BlockSpec((None,) + a.shape[1:], lambda *_: (layer, 0, 0), pipeline_mode=pl.Buffered(1))
    if multi:
        grid = (1, 1)
        rows = lambda width: pl.BlockSpec((tm, width), lambda b, t: (0, 0))
        hg_spec = pl.BlockSpec(hg.shape, lambda b, t: (0, 0, 0))
        gt_shape = jax.ShapeDtypeStruct((n, D_FF), F32)
        gt_spec = rows(D_FF)
    else:
        nt = seq // tm
        grid = (batch, nt)
        rows = lambda width: pl.BlockSpec((tm, width), lambda b, t: (b * nt + t, 0))
        hg_spec = pl.BlockSpec((None, SUBLANES, D_FF), lambda b, t: (b, 0, 0))
        gt_shape = jax.ShapeDtypeStruct((batch, SUBLANES, D_FF), F32)
        gt_spec = pl.BlockSpec((None, SUBLANES, D_FF), lambda b, t: (b, 0, 0))
    return pl.pallas_call(
        functools.partial(_post_kernel, tm=tm, seg=min(seq, tm)),
        grid=grid,
        in_specs=[rows(D_MODEL), rows(D_ATT), rows(D_LRU), rows(D_ATT), hg_spec, full(p["g"]),
                  layer_of(w["wout"]), layer_of(w["wup"]), full(p["fcw"]), full(p["fcb"]), layer_of(w["wdown"])],
        out_specs=(rows(D_MODEL), gt_spec),
        out_shape=(jax.ShapeDtypeStruct((n, D_MODEL), F32), gt_shape),
        scratch_shapes=[pltpu.VMEM((SUBLANES, D_FF), F32)],
        compiler_params=_cparams(("arbitrary", "arbitrary")),
        name="post",
    )(x, of, ol, ob, hg, p["g"], w["wout"], w["wup"], p["fcw"], p["fcb"], w["wdown"])


def _block_diag_halves(w):
    nb, bw, _ = w.shape
    hb = nb // 2
    eye = jnp.eye(hb, dtype=w.dtype)
    w = w.reshape(2, hb, bw, bw)
    return (eye[None, :, None, :, None] * w[:, :, :, None, :]).reshape(2, hb * bw, hb * bw)


def _prep_layer(norm_g, w_in_t, b_forget, lru_conv_w, lru_conv_b, lru_wa, lru_ba, lru_wx, lru_bx,
                lru_lambda, rel_bias, ffn_conv_w, ffn_conv_b):
    o = 0
    wq, wk, wv = (w_in_t[o + j * D_ATT:o + (j + 1) * D_ATT] for j in range(3)); o += 3 * D_ATT
    wf = w_in_t[o:o + N_HEADS]; o += N_HEADS
    wlru = w_in_t[o:o + 2 * D_LRU]; o += 2 * D_LRU
    wqb, wkb, wvb = (w_in_t[o + j * D_ATT:o + (j + 1) * D_ATT] for j in range(3))
    wf = jnp.pad(wf, ((0, SUBLANES - N_HEADS), (0, 0)))
    return dict(
        g=norm_g,
        wq=wq, wk=wk, wv=wv, wf=wf,
        wkv=jnp.concatenate([wk, wv, wkb, wvb, wf], axis=0),
        bf=jnp.broadcast_to(jnp.pad(b_forget, (0, SUBLANES - N_HEADS))[:, None], (SUBLANES, LANES)),
        wlru=wlru,
        wqb=wqb, wkb=wkb, wvb=wvb,
        cw=lru_conv_w, cb=lru_conv_b.reshape(1, D_LRU),
        wa=_block_diag_halves(lru_wa).astype(BF16), ba=lru_ba.reshape(1, D_LRU),
        wx=_block_diag_halves(lru_wx).astype(BF16), bx=lru_bx.reshape(1, D_LRU),
        lam=lru_lambda.reshape(1, D_LRU),
        rel=rel_bias,
        fcw=ffn_conv_w, fcb=ffn_conv_b.reshape(1, D_FF),
    )


def _prompt_layer(x, p, w, bm, layer, batch, seq):
    zeros8 = jnp.zeros((batch, SUBLANES, D_LRU), F32)
    (qf, kft, vft, kfa, vfa, lft, o_lru, hl, xt, qb, kba, vba, kbt, vbt) = _pre_call(x, zeros8, zeros8, p, batch, seq)
    o_fox = _fox_call(qf, kfa, vfa, batch, seq)
    o_band = _band_call(qb, kba, vba, bm, batch, seq)
    y, gt = _post_call(x, o_fox, o_lru, o_band, jnp.zeros((batch, SUBLANES, D_FF), F32), p, w, layer, batch, seq,
                       TILE)
    state = (kft, vft, lft, kbt, vbt, hl[:, SUBLANES - 1], xt[:, SUBLANES - (LRU_CONV - 1):],
             gt[:, SUBLANES - (FFN_CONV - 1):])
    return y, state


def _sample_layer(x, p, w, bm, layer, batch, t, fox_kt, fox_vt, fox_lf, band_kt, band_vt, lru_h0, lru_conv_h,
                  ffn_conv_h):
    n = batch * t
    (qf, kf, vf, kfn, vfn, lft, xb, yb, qb, kb, vb, kbn, vbn) = _pre_s_call(x, p, batch, t)
    lf_new = jnp.pad(lft.reshape(SUBLANES, batch, t).transpose(1, 0, 2), ((0, 0), (0, 0), (0, LANES - t)))
    o_fox = _fox_s_call(qf, fox_kt, fox_vt, fox_lf, kfn, vfn, lf_new, layer, batch, t)

    hx = jnp.stack([jnp.pad(lru_conv_h[:, j:], ((0, 0), (0, t - (LRU_CONV - 1 - j)), (0, 0))).reshape(n, D_LRU)
                    for j in range(LRU_CONV - 1)])
    h0 = jnp.repeat(lru_h0, t, axis=0)
    o_lru, hl = _lru_s_call(xb, yb, hx, h0, p, t)
    o_band = _band_s_call(qb, band_kt, band_vt, kbn, vbn, bm, layer, batch, t)
    hg = jnp.stack([jnp.pad(ffn_conv_h[:, j:], ((0, 0), (0, t - (FFN_CONV - 1 - j)), (0, 0))).reshape(n, D_FF)
                    for j in range(FFN_CONV - 1)])
    y, gt = _post_call(x, o_fox, o_lru, o_band, hg, p, w, layer, batch, t, n)

    state = (
        kf.reshape(batch, t, N_HEADS, HEAD_DIM),
        vf.reshape(batch, t, N_HEADS, HEAD_DIM),
        lft.reshape(SUBLANES, batch, t)[:N_HEADS].transpose(1, 2, 0),
        kb.reshape(batch, t, N_HEADS, HEAD_DIM),
        vb.reshape(batch, t, N_HEADS, HEAD_DIM),
        hl.reshape(batch, t, D_LRU)[:, t - 1],
        xb.reshape(batch, t, D_LRU)[:, t - (LRU_CONV - 1):],
        gt.reshape(batch, t, D_FF)[:, t - (FFN_CONV - 1):],
    )
    return y, state


def _feature_major(c):
    d, b, s = c.shape[:3]
    return c.transpose(0, 1, 3, 4, 2).reshape(d, b, D_ATT, s)


def _token_major(c):
    d, b, _, s = c.shape
    return c.reshape(d, b, N_HEADS, HEAD_DIM, s).transpose(0, 1, 4, 2, 3)


def kernel(x_prompt, x_sample, cache_fox_k, cache_fox_v, cache_fox_logf, cache_band_k, cache_band_v, state_lru_h, state_lru_conv, state_ffn_conv, norm_g, w_in, b_forget, lru_conv_w, lru_conv_b, lru_wa, lru_ba, lru_wx, lru_bx, lru_lambda, rel_bias, w_out, w_up, ffn_conv_w, ffn_conv_b, w_down):
    bp, seq, _ = x_prompt.shape
    bs, t, _ = x_sample.shape
    depth = norm_g.shape[0]
    assert seq % TILE == 0 and t < LANES and (t & (t - 1)) == 0 and cache_band_k.shape[2] == BAND_KEEP
    assert cache_fox_k.shape[2] % LANES == 0 and bs % SAMPLE_GROUP == 0
    xp = x_prompt.reshape(bp * seq, D_MODEL)
    xs = x_sample.reshape(bs * t, D_MODEL)
    w_in_t = w_in.transpose(0, 2, 1).astype(BF16)
    w = dict(wout=w_out.astype(BF16), wup=w_up.astype(BF16), wdown=w_down.astype(BF16))
    fox_kt, fox_vt = _feature_major(cache_fox_k), _feature_major(cache_fox_v)
    band_kt, band_vt = _feature_major(cache_band_k), _feature_major(cache_band_v)
    fox_lf = cache_fox_logf.transpose(0, 1, 3, 2)
    st_p, st_s = [], []
    for l in range(depth):
        p = _prep_layer(norm_g[l], w_in_t[l], b_forget[l], lru_conv_w[l], lru_conv_b[l], lru_wa[l], lru_ba[l],
                        lru_wx[l], lru_bx[l], lru_lambda[l], rel_bias[l], ffn_conv_w[l], ffn_conv_b[l])
        bm = _bias_call(p["rel"])
        xp, new_p = _prompt_layer(xp, p, w, bm, l, bp, seq)
        xs, new_s = _sample_layer(xs, p, w, bm, l, bs, t, fox_kt, fox_vt, fox_lf, band_kt, band_vt,
                                  state_lru_h[l], state_lru_conv[l], state_ffn_conv[l])
        st_p.append(new_p)
        st_s.append(new_s)

    stack = lambda states, j: jnp.stack([st[j] for st in states], axis=0)
    outs = [xp.reshape(bp, seq, D_MODEL), xs.reshape(bs, t, D_MODEL)]
    for j in range(8):
        sp, ss = stack(st_p, j), stack(st_s, j)
        if j in (0, 1, 3, 4):
            sp = _token_major(sp)
        elif j == 2:
            sp = sp[:, :, :N_HEADS].transpose(0, 1, 3, 2)
        outs += [sp, ss]
    return tuple(outs)
```

```python
import functools

import jax
import jax.numpy as jnp
from jax import lax
from jax.experimental import pallas as pl
from jax.experimental.pallas import tpu as pltpu

D_MODEL = 1024
CHUNK = 64
HEAD_DIM = 64
N_HEADS = 4
D_ATT = N_HEADS * HEAD_DIM
D_LRU = 512
LRU_HALF = D_LRU // 2
LRU_CONV = 4
RGLRU_C = 8.0
BAND_PREV = 8
BAND_KEEP = BAND_PREV * CHUNK
REL_CLIP = 128
N_REL = REL_CLIP + CHUNK
D_FF = 3 * D_MODEL
FF_CHUNK = 1536
FFN_CONV = 3
RMS_EPS = 1e-6
NEG_INF = -1e30
ATTN_SCALE = HEAD_DIM ** -0.5
LOG2E = 1.4426950408889634

LANES = 128
SUBLANES = 8
HEAD_PAD = 80
KV_ROWS = N_HEADS * HEAD_PAD
Q_COLS = N_HEADS * LANES
N_CPARTS = 3
TILE = 512
SAMPLE_GROUP = 4
BAND_TQ = 4 * CHUNK
BAND_TK = BAND_TQ + BAND_KEEP
VMEM_LIMIT = 56 * 1024 * 1024

BF16 = jnp.bfloat16
F32 = jnp.float32


def _cparams(sem):
    return pltpu.CompilerParams(dimension_semantics=sem, vmem_limit_bytes=VMEM_LIMIT)


def _rms(x, g):
    y = x * lax.rsqrt(jnp.mean(x * x, axis=-1, keepdims=True) + RMS_EPS)
    return y * g


def _log_sigmoid(x):
    return jnp.minimum(x, 0.0) - jnp.log1p(jnp.exp(-jnp.abs(x)))


def _gelu(x):
    return jax.nn.gelu(x)


def _dot(a, b):
    return jnp.dot(a, b, preferred_element_type=F32)


def _dot_nt(a, b):
    return lax.dot_general(a, b, (((1,), (1,)), ((), ())), preferred_element_type=F32)


def _cumsum_lanes(x):
    n = x.shape[-1]
    lane = lax.broadcasted_iota(jnp.int32, x.shape, x.ndim - 1)
    s = 1
    while s < n:
        x = x + jnp.where(lane >= s, pltpu.roll(x, s, x.ndim - 1), 0.0)
        s *= 2
    return x


def _rows(vals, n, width):
    r = lax.broadcasted_iota(jnp.int32, (n, width), 0)
    out = jnp.zeros((n, width), F32)
    for j, v in enumerate(vals):
        out = jnp.where(r == j, v, out)
    return out


def _shift_rows(x3, tail, d):
    r = pltpu.roll(x3, d, 1)
    prev = jnp.concatenate([pltpu.roll(tail[None], d, 1), r[:-1]], axis=0)
    sub = lax.broadcasted_iota(jnp.int32, x3.shape, 1)
    return jnp.where(sub >= d, r, prev)


def _gate_logits(xc, wa_ref, wx_ref):
    xcb = xc.astype(BF16)

    def block_diag(w_ref):
        return jnp.concatenate([_dot(xcb[:, :LRU_HALF], w_ref[0]), _dot(xcb[:, LRU_HALF:], w_ref[1])], axis=1)

    return block_diag(wa_ref), block_diag(wx_ref)


def _gate_terms(logits, xc, ba, bx, lam):
    r = jax.nn.sigmoid(logits[0] + ba)
    ig = jax.nn.sigmoid(logits[1] + bx)
    log_a = RGLRU_C * r * _log_sigmoid(lam)
    a = jnp.exp(log_a)
    th = jnp.tanh(log_a)
    u = (jnp.sqrt(-2.0 * th) * lax.rsqrt(1.0 - th)) * (ig * xc)
    return a, u


def _q_tiles(q, helper):
    qr = pltpu.roll(q, HEAD_DIM, 1)
    lane = lax.broadcasted_iota(jnp.int32, (1, LANES), 1)
    tiles = [q[:, 0:LANES], qr[:, LANES:2 * LANES], q[:, LANES:2 * LANES], qr[:, 0:LANES]]
    return jnp.concatenate([jnp.where(lane < HEAD_DIM, t, helper) for t in tiles], axis=1).astype(BF16)


def _pre_kernel(x_ref, g_ref, wfm_ref, bf_ref, wlru_ref, wqb_ref,
                hx_ref, h0_ref, cw_ref, cb_ref, wa_ref, ba_ref, wx_ref, bx_ref, lam_ref,
                qfa_ref, kft_ref, vft_ref, kfa_ref, vfa_ref, lft_ref, ol_ref, hl_ref, xt_ref,
                qb_ref, kba_ref, vba_ref, kbt_ref, vbt_ref, ccar_ref, xtail_ref, hcar_ref, *, tm, nt):
    t = pl.program_id(1)
    helper = HEAD_PAD - HEAD_DIM
    groups = tm // SUBLANES

    @pl.when(t == 0)
    def _():
        ccar_ref[...] = jnp.zeros_like(ccar_ref)
        xtail_ref[...] = hx_ref[...]
        hcar_ref[...] = h0_ref[...]
        flag = _rows([jnp.full((1, tm), NEG_INF, F32)], helper, tm).astype(BF16)
        for h in range(N_HEADS):
            r0 = h * HEAD_PAD
            kba_ref[r0:r0 + HEAD_DIM, :] = jnp.zeros((HEAD_DIM, tm), BF16)
            kba_ref[r0 + HEAD_DIM:r0 + HEAD_PAD, :] = flag
        vba_ref[...] = jnp.zeros_like(vba_ref)

    @pl.when(t > 0)
    def _():
        h = _rms(x_ref[...], g_ref[...]).astype(BF16)
        lru = _dot_nt(h, wlru_ref[...])
        xb = lru[:, :D_LRU]
        yb = lru[:, D_LRU:]

        x3 = xb.reshape(groups, SUBLANES, D_LRU)
        tail = xtail_ref[...]
        xc = _shift_rows(x3, tail, 3) * cw_ref[0:1]
        xc = xc + _shift_rows(x3, tail, 2) * cw_ref[1:2]
        xc = xc + _shift_rows(x3, tail, 1) * cw_ref[2:3]
        xc = xc + x3 * cw_ref[3:4]
        xc = (xc + cb_ref[...]).reshape(tm, D_LRU)

        lane = lax.broadcasted_iota(jnp.int32, (1, LANES), 1)
        ones_b = jnp.where(lane == HEAD_DIM, 1.0, 0.0)
        qb_ref[...] = _q_tiles(_dot_nt(h, wqb_ref[...]) * ATTN_SCALE, ones_b)

        fm = _dot_nt(wfm_ref[...], h)
        qt, kt, vt, kbt, vbt = (fm[j * D_ATT:(j + 1) * D_ATT] for j in range(5))
        lf = _log_sigmoid(fm[5 * D_ATT:] + bf_ref[:, 0:1])
        lft_ref[...] = lf
        c = _cumsum_lanes(lf) + ccar_ref[:, 0:1]
        ccar_ref[...] = jnp.broadcast_to(c[:, tm - 1:tm], ccar_ref.shape)
        c = c * LOG2E
        c1 = c.astype(BF16)
        r1 = c - c1.astype(F32)
        c2 = r1.astype(BF16)
        c3 = (r1 - c2.astype(F32)).astype(BF16)
        ones_row = _rows([jnp.ones((1, tm), F32)], helper, tm).astype(BF16)
        zero_rows = jnp.zeros((helper, tm), BF16)

        kft_ref[...] = kt
        vft_ref[...] = vt
        ones_q = _rows([jnp.ones((1, tm), F32)] * N_CPARTS, helper, tm).astype(BF16)
        k_tiles = []
        for hh in range(N_HEADS):
            r0 = hh * HEAD_PAD
            f0 = hh * HEAD_DIM
            qfa_ref[r0:r0 + HEAD_DIM, :] = (qt[f0:f0 + HEAD_DIM] * (ATTN_SCALE * LOG2E)).astype(BF16)
            qfa_ref[r0 + HEAD_DIM:r0 + HEAD_PAD, :] = ones_q
            neg_c = [-(p[hh:hh + 1].astype(F32)) for p in (c1, c2, c3)]
            tile = jnp.concatenate([kt[f0:f0 + HEAD_DIM], _rows(neg_c, LANES - HEAD_DIM, tm)], axis=0)
            k_tiles.append(tile.T.astype(BF16))
            vfa_ref[r0:r0 + HEAD_DIM, :] = vt[f0:f0 + HEAD_DIM].astype(BF16)
            vfa_ref[r0 + HEAD_DIM:r0 + HEAD_PAD, :] = ones_row
        kfa_ref[...] = jnp.concatenate(k_tiles, axis=1)
        for hh in range(N_HEADS):
            r0 = hh * HEAD_PAD
            f0 = hh * HEAD_DIM
            kba_ref[r0:r0 + HEAD_DIM, :] = kbt[f0:f0 + HEAD_DIM].astype(BF16)
            kba_ref[r0 + HEAD_DIM:r0 + HEAD_PAD, :] = zero_rows
            vba_ref[r0:r0 + HEAD_DIM, :] = vbt[f0:f0 + HEAD_DIM].astype(BF16)
            vba_ref[r0 + HEAD_DIM:r0 + HEAD_PAD, :] = ones_row

        logits = _gate_logits(xc, wa_ref, wx_ref)
        a, u = _gate_terms(logits, xc, ba_ref[...], bx_ref[...], lam_ref[...])
        a = a.reshape(groups, SUBLANES, D_LRU)
        u = u.reshape(groups, SUBLANES, D_LRU)
        sub = lax.broadcasted_iota(jnp.int32, a.shape, 1)
        s = 1
        while s < SUBLANES:
            ok = sub >= s
            u = jnp.where(ok, a * pltpu.roll(u, s, 1) + u, u)
            a = jnp.where(ok, a * pltpu.roll(a, s, 1), a)
            s *= 2
        hprev = hcar_ref[SUBLANES - 1:SUBLANES]
        hs = []
        for gi in range(groups):
            hg = a[gi] * hprev + u[gi]
            hs.append(hg)
            hprev = hg[SUBLANES - 1:SUBLANES]
        hseq = jnp.concatenate(hs, axis=0)
        ol_ref[...] = (hseq * _gelu(yb)).astype(BF16)
        hl_ref[...] = hs[-1]
        hcar_ref[...] = hs[-1]
        xt_ref[...] = xb[tm - SUBLANES:]
        xtail_ref[...] = xb[tm - SUBLANES:]

        @pl.when(t == nt)
        def _():
            kbt_ref[...] = kbt
            vbt_ref[...] = vbt


def _pre_call(x, hx, h0, p, batch, seq):
    tm = TILE
    assert tm == BAND_KEEP
    nt = seq // tm
    n = batch * seq
    tok = lambda w: pl.BlockSpec((tm, w), lambda b, t: (b * nt + jnp.maximum(t - 1, 0), 0))
    feat = lambda r: pl.BlockSpec((None, r, tm), lambda b, t: (b, 0, jnp.maximum(t - 1, 0)))
    padded = pl.BlockSpec((None, KV_ROWS, tm), lambda b, t: (b, 0, t))
    last = pl.BlockSpec((None, D_ATT, tm), lambda b, t: (b, 0, 0))
    state = pl.BlockSpec((None, SUBLANES, D_LRU), lambda b, t: (b, 0, 0))
    full = lambda a: pl.BlockSpec(a.shape, lambda b, t: (0,) * a.ndim)
    sd = jax.ShapeDtypeStruct
    ws1 = [p["g"][0:1], p["wfm"], p["bf"], p["wlru"], p["wqb"]]
    ws2 = [p["cw"], p["cb"], p["wa"], p["ba"], p["wx"], p["bx"], p["lam"]]
    st = sd((batch, SUBLANES, D_LRU), F32)
    out_shape = (
        sd((batch, KV_ROWS, seq), BF16), sd((batch, D_ATT, seq), F32), sd((batch, D_ATT, seq), F32),
        sd((n, Q_COLS), BF16), sd((batch, KV_ROWS, seq), BF16),
        sd((batch, SUBLANES, seq), F32), sd((n, D_LRU), BF16), st, st,
        sd((n, Q_COLS), BF16), sd((batch, KV_ROWS, seq + tm), BF16), sd((batch, KV_ROWS, seq + tm), BF16),
        sd((batch, D_ATT, tm), F32), sd((batch, D_ATT, tm), F32),
    )
    out_specs = (
        feat(KV_ROWS), feat(D_ATT), feat(D_ATT), tok(Q_COLS), feat(KV_ROWS),
        feat(SUBLANES), tok(D_LRU), state, state,
        tok(Q_COLS), padded, padded, last, last,
    )
    return pl.pallas_call(
        functools.partial(_pre_kernel, tm=tm, nt=nt),
        grid=(batch, nt + 1),
        in_specs=[tok(D_MODEL)] + [full(w) for w in ws1] + [state, state] + [full(w) for w in ws2],
        out_specs=out_specs,
        out_shape=out_shape,
        scratch_shapes=[pltpu.VMEM((SUBLANES, LANES), F32), pltpu.VMEM((SUBLANES, D_LRU), F32),
                        pltpu.VMEM((SUBLANES, D_LRU), F32)],
        compiler_params=_cparams(("arbitrary", "arbitrary")),
        name="pre",
    )(x, *ws1, hx, h0, *ws2)


def _pre_s_kernel(x_ref, g_ref, wq_ref, wk_ref, wv_ref, wf_ref, bf_ref, wlru_ref, wqb_ref, wkb_ref, wvb_ref,
                  qf_ref, kf_ref, vf_ref, kfn_ref, vfn_ref, lft_ref, xb_ref, yb_ref,
                  qb_ref, kb_ref, vb_ref, kbn_ref, vbn_ref, *, batch, t):
    h = _rms(x_ref[...], g_ref[...]).astype(BF16)
    qf_ref[...] = (_dot_nt(h, wq_ref[...]) * ATTN_SCALE).astype(BF16)
    qb_ref[...] = (_dot_nt(h, wqb_ref[...]) * ATTN_SCALE).astype(BF16)
    lft_ref[...] = _log_sigmoid(_dot_nt(wf_ref[...], h) + bf_ref[:, 0:1])
    lru = _dot_nt(h, wlru_ref[...])
    xb_ref[...] = lru[:, :D_LRU]
    yb_ref[...] = lru[:, D_LRU:]
    for w_ref, o_ref, on_ref in ((wk_ref, kf_ref, kfn_ref), (wv_ref, vf_ref, vfn_ref),
                                 (wkb_ref, kb_ref, kbn_ref), (wvb_ref, vb_ref, vbn_ref)):
        y = _dot_nt(h, w_ref[...])
        o_ref[...] = y
        on_ref[...] = jnp.zeros_like(on_ref)
        yb16 = y.astype(BF16)
        for b in range(batch):
            on_ref[b, 0:t, :] = yb16[b * t:(b + 1) * t]


def _pre_s_call(x, p, batch, t):
    n = batch * t
    sd = jax.ShapeDtypeStruct
    ws = [p["g"][0:1], p["wq"], p["wk"], p["wv"], p["wf"], p["bf"], p["wlru"], p["wqb"], p["wkb"], p["wvb"]]
    new = sd((batch, LANES, D_ATT), BF16)
    tokm = sd((n, D_ATT), F32)
    out_shape = (
        sd((n, D_ATT), BF16), tokm, tokm, new, new, sd((SUBLANES, n), F32),
        sd((n, D_LRU), F32), sd((n, D_LRU), F32),
        sd((n, D_ATT), BF16), tokm, tokm, new, new,
    )
    return pl.pallas_call(
        functools.partial(_pre_s_kernel, batch=batch, t=t),
        out_shape=out_shape,
        compiler_params=pltpu.CompilerParams(vmem_limit_bytes=VMEM_LIMIT),
        name="pre_s",
    )(x, *ws)


def _fox_kernel(q_ref, k_ref, v_ref, o_ref, sa_ref, sb_ref, *, tb):
    i = pl.program_id(1)
    row = lax.broadcasted_iota(jnp.int32, (tb, tb), 0)
    col = lax.broadcasted_iota(jnp.int32, (tb, tb), 1)
    causal = row <= col
    heads = range(N_HEADS)
    q = [q_ref[h * HEAD_PAD:(h + 1) * HEAD_PAD, :] for h in heads]

    def scores(kb, s_ref):
        start = pl.multiple_of(kb * tb, tb)
        for h in heads:
            s_ref[h] = _dot(k_ref[pl.ds(start, tb), h * LANES:h * LANES + HEAD_PAD], q[h])

    def block(kb, s_ref, carry, masked):
        start = pl.multiple_of(kb * tb, tb)
        ps, alphas, ms = [], [], []
        for h in heads:
            sh = s_ref[h]
            if masked:
                sh = jnp.where(causal, sh, NEG_INF)
            m = carry[2 * h]
            m_new = jnp.maximum(m, jnp.max(sh, axis=0, keepdims=True))
            alphas.append(jnp.exp2(m - m_new))
            ps.append(jnp.exp2(sh - m_new).astype(BF16))
            ms.append(m_new)
        out = []
        for h in heads:
            pv = _dot(v_ref[h * HEAD_PAD:(h + 1) * HEAD_PAD, pl.ds(start, tb)], ps[h])
            out += [ms[h], alphas[h] * carry[2 * h + 1] + pv]
        return tuple(out)

    init = []
    for h in heads:
        init += [jnp.full((1, tb), NEG_INF, F32), jnp.zeros((HEAD_PAD, tb), F32)]
    scores(0, sa_ref)

    def pair(j, carry):
        kb = 2 * j
        scores(kb + 1, sb_ref)
        carry = block(kb, sa_ref, carry, False)
        scores(kb + 2, sa_ref)
        return block(kb + 1, sb_ref, carry, False)

    carry = lax.fori_loop(0, i // 2, pair, tuple(init))

    def odd(carry):
        scores(i, sb_ref)
        carry = block(i - 1, sa_ref, carry, False)
        return block(i, sb_ref, carry, True)

    def even(carry):
        return block(i, sa_ref, carry, True)

    carry = lax.cond((i & 1) == 1, odd, even, carry)
    outs = []
    for h in heads:
        acc = carry[2 * h + 1]
        outs.append(acc[:HEAD_DIM] / acc[HEAD_DIM:HEAD_DIM + 1])
    o_ref[...] = jnp.concatenate(outs, axis=0).T.astype(BF16)


def _fox_call(q, k, v, batch, seq):
    tb = TILE
    nq = seq // tb
    return pl.pallas_call(
        functools.partial(_fox_kernel, tb=tb),
        grid=(batch, nq),
        in_specs=[
            pl.BlockSpec((None, KV_ROWS, tb), lambda b, i: (b, 0, i)),
            pl.BlockSpec((seq, Q_COLS), lambda b, i: (b, 0)),
            pl.BlockSpec((None, KV_ROWS, seq), lambda b, i: (b, 0, 0)),
        ],
        out_specs=pl.BlockSpec((tb, D_ATT), lambda b, i: (b * nq + i, 0)),
        out_shape=jax.ShapeDtypeStruct((batch * seq, D_ATT), BF16),
        scratch_shapes=[pltpu.VMEM((N_HEADS, tb, tb), F32), pltpu.VMEM((N_HEADS, tb, tb), F32)],
        compiler_params=_cparams(("arbitrary", "arbitrary")),
        name="fox",
    )(q, k, v)


def _fox_s_kernel(q_ref, kh_ref, vh_ref, lfh_ref, kn_ref, vn_ref, lfn_ref, o_ref, *, t, gb):
    row = lax.broadcasted_iota(jnp.int32, (t, LANES), 0)
    col = lax.broadcasted_iota(jnp.int32, (t, LANES), 1)
    causal = col <= row
    for g in range(gb):
        rs = slice(g * t, (g + 1) * t)
        c_h = _cumsum_lanes(lfh_ref[g])
        past = c_h.shape[-1]
        c_n = c_h[:, past - 1:past] + _cumsum_lanes(lfn_ref[g, 0:N_HEADS, :])
        for h in range(N_HEADS):
            fs = slice(h * HEAD_DIM, (h + 1) * HEAD_DIM)
            q = q_ref[rs, fs]
            s_h = _dot(q, kh_ref[g, fs, :].astype(BF16)) - c_h[h:h + 1]
            s_n = _dot_nt(q, kn_ref[g, :, fs]) - c_n[h:h + 1]
            s_n = jnp.where(causal, s_n, NEG_INF)
            m = jnp.maximum(jnp.max(s_h, axis=-1, keepdims=True), jnp.max(s_n, axis=-1, keepdims=True))
            p_h = jnp.exp(s_h - m)
            p_n = jnp.exp(s_n - m)
            l = jnp.sum(p_h, axis=-1, keepdims=True) + jnp.sum(p_n, axis=-1, keepdims=True)
            o = (_dot_nt(p_h.astype(BF16), vh_ref[g, fs, :].astype(BF16))
                 + _dot(p_n.astype(BF16), vn_ref[g, :, fs]))
            o_ref[rs, fs] = (o / l).astype(BF16)


def _fox_s_call(q, kt_hist, vt_hist, lf_hist, k_new, v_new, lf_new, layer, batch, t):
    past = kt_hist.shape[-1]
    gb = SAMPLE_GROUP
    hist = pl.BlockSpec((None, gb, D_ATT, past), lambda j: (layer, j, 0, 0))
    new = pl.BlockSpec((gb, LANES, D_ATT), lambda j: (j, 0, 0))
    return pl.pallas_call(
        functools.partial(_fox_s_kernel, t=t, gb=gb),
        grid=(batch // gb,),
        in_specs=[
            pl.BlockSpec((gb * t, D_ATT), lambda j: (j, 0)),
            hist, hist,
            pl.BlockSpec((None, gb, N_HEADS, past), lambda j: (layer, j, 0, 0)),
            new, new,
            pl.BlockSpec((gb, SUBLANES, LANES), lambda j: (j, 0, 0)),
        ],
        out_specs=pl.BlockSpec((gb * t, D_ATT), lambda j: (j, 0)),
        out_shape=jax.ShapeDtypeStruct((batch * t, D_ATT), BF16),
        compiler_params=_cparams(("arbitrary",)),
        name="fox_s",
    )(q, kt_hist, vt_hist, lf_hist, k_new, v_new, lf_new)


def _lru_s_kernel(xb_ref, yb_ref, hx_ref, h0_ref, cw_ref, cb_ref, wa_ref, ba_ref, wx_ref, bx_ref,
                  lam_ref, o_ref, hl_ref, *, seg):
    x = xb_ref[...]
    rmod = lax.broadcasted_iota(jnp.int32, x.shape, 0) & (seg - 1)
    prevs = [jnp.where(rmod >= d, pltpu.roll(x, d, 0), hx_ref[LRU_CONV - 1 - d]) for d in range(1, LRU_CONV)]
    xc = prevs[2] * cw_ref[0:1]
    xc = xc + prevs[1] * cw_ref[1:2]
    xc = xc + prevs[0] * cw_ref[2:3]
    xc = xc + x * cw_ref[3:4]
    xc = xc + cb_ref[...]
    a, u = _gate_terms(_gate_logits(xc, wa_ref, wx_ref), xc, ba_ref[...], bx_ref[...], lam_ref[...])
    s = 1
    while s < seg:
        ok = rmod >= s
        u = jnp.where(ok, a * pltpu.roll(u, s, 0) + u, u)
        a = jnp.where(ok, a * pltpu.roll(a, s, 0), a)
        s *= 2
    hseq = a * h0_ref[...] + u
    o_ref[...] = (hseq * _gelu(yb_ref[...])).astype(BF16)
    hl_ref[...] = hseq


def _lru_s_call(xb, yb, hx, h0, p, seg):
    n = xb.shape[0]
    return pl.pallas_call(
        functools.partial(_lru_s_kernel, seg=seg),
        out_shape=(jax.ShapeDtypeStruct((n, D_LRU), BF16), jax.ShapeDtypeStruct((n, D_LRU), F32)),
        compiler_params=pltpu.CompilerParams(vmem_limit_bytes=VMEM_LIMIT),
        name="lru_s",
    )(xb, yb, hx, h0, p["cw"], p["cb"], p["wa"], p["ba"], p["wx"], p["bx"], p["lam"])


def _bias_kernel(u_ref, o_ref):
    r = lax.broadcasted_iota(jnp.int32, (BAND_TQ, BAND_TK), 0)
    c = lax.broadcasted_iota(jnp.int32, (BAND_TQ, BAND_TK), 1)
    dch = (c >> 6) - (r >> 6)
    inband = (dch >= 0) & (dch <= BAND_PREV)
    for h in range(N_HEADS):
        ub = jnp.broadcast_to(u_ref[h:h + 1, :], (BAND_TQ, BAND_TK))
        o_ref[h] = jnp.where(inband, pltpu.roll(ub, 0, 1, stride=1, stride_axis=0), NEG_INF)


def _bias_call(rel):
    left = BAND_KEEP - REL_CLIP
    u = jnp.concatenate([jnp.broadcast_to(rel[:, 0:1], (N_HEADS, left)), rel,
                         jnp.broadcast_to(rel[:, 0:1], (N_HEADS, BAND_TK - left - N_REL))], axis=1)
    return pl.pallas_call(
        _bias_kernel,
        out_shape=jax.ShapeDtypeStruct((N_HEADS, BAND_TQ, BAND_TK), F32),
        compiler_params=pltpu.CompilerParams(vmem_limit_bytes=VMEM_LIMIT),
        name="band_bias",
    )(u)


def _band_kernel(q_ref, k_ref, v_ref, bm_ref, o_ref):
    i = pl.program_id(1)
    start = pl.multiple_of(i * BAND_TQ, BAND_TQ)
    heads = range(N_HEADS)
    s = [_dot(q_ref[:, h * LANES:h * LANES + HEAD_PAD],
              k_ref[h * HEAD_PAD:(h + 1) * HEAD_PAD, pl.ds(start, BAND_TK)]) + bm_ref[h] for h in heads]
    ps = []
    for h in heads:
        m = jnp.max(s[h], axis=-1, keepdims=True)
        ps.append(jnp.exp(s[h] - m).astype(BF16))
    for h in heads:
        ov = _dot_nt(ps[h], v_ref[h * HEAD_PAD:(h + 1) * HEAD_PAD, pl.ds(start, BAND_TK)])
        o_ref[:, h * HEAD_DIM:(h + 1) * HEAD_DIM] = (ov[:, :HEAD_DIM] / ov[:, HEAD_DIM:HEAD_DIM + 1]).astype(BF16)


def _band_call(q, kpad, vpad, bm, batch, seq):
    nq = seq // BAND_TQ
    lk = seq + BAND_KEEP
    return pl.pallas_call(
        _band_kernel,
        grid=(batch, nq),
        in_specs=[
            pl.BlockSpec((BAND_TQ, Q_COLS), lambda b, i: (b * nq + i, 0)),
            pl.BlockSpec((None, KV_ROWS, lk), lambda b, i: (b, 0, 0)),
            pl.BlockSpec((None, KV_ROWS, lk), lambda b, i: (b, 0, 0)),
            pl.BlockSpec(bm.shape, lambda b, i: (0, 0, 0)),
        ],
        out_specs=pl.BlockSpec((BAND_TQ, D_ATT), lambda b, i: (b * nq + i, 0)),
        out_shape=jax.ShapeDtypeStruct((batch * seq, D_ATT), BF16),
        compiler_params=_cparams(("arbitrary", "arbitrary")),
        name="band",
    )(q, kpad, vpad, bm)


def _band_s_kernel(q_ref, kh_ref, vh_ref, kn_ref, vn_ref, bm_ref, o_ref, *, t, gb):
    col = lax.broadcasted_iota(jnp.int32, (t, LANES), 1)
    exists = col < t
    for g in range(gb):
        rs = slice(g * t, (g + 1) * t)
        for h in range(N_HEADS):
            fs = slice(h * HEAD_DIM, (h + 1) * HEAD_DIM)
            q = q_ref[rs, fs]
            s_h = _dot(q, kh_ref[g, fs, :].astype(BF16)) + bm_ref[h, :t, :BAND_KEEP]
            s_n = _dot_nt(q, kn_ref[g, :, fs]) + bm_ref[h, :t, BAND_KEEP:BAND_KEEP + LANES]
            s_n = jnp.where(exists, s_n, NEG_INF)
            m = jnp.maximum(jnp.max(s_h, axis=-1, keepdims=True), jnp.max(s_n, axis=-1, keepdims=True))
            p_h = jnp.exp(s_h - m)
            p_n = jnp.exp(s_n - m)
            l = jnp.sum(p_h, axis=-1, keepdims=True) + jnp.sum(p_n, axis=-1, keepdims=True)
            o = (_dot_nt(p_h.astype(BF16), vh_ref[g, fs, :].astype(BF16))
                 + _dot(p_n.astype(BF16), vn_ref[g, :, fs]))
            o_ref[rs, fs] = (o / l).astype(BF16)


def _band_s_call(q, kt_hist, vt_hist, k_new, v_new, bm, layer, batch, t):
    gb = SAMPLE_GROUP
    hist = pl.BlockSpec((None, gb, D_ATT, BAND_KEEP), lambda j: (layer, j, 0, 0))
    new = pl.BlockSpec((gb, LANES, D_ATT), lambda j: (j, 0, 0))
    return pl.pallas_call(
        functools.partial(_band_s_kernel, t=t, gb=gb),
        grid=(batch // gb,),
        in_specs=[pl.BlockSpec((gb * t, D_ATT), lambda j: (j, 0)), hist, hist, new, new,
                  pl.BlockSpec(bm.shape, lambda j: (0, 0, 0))],
        out_specs=pl.BlockSpec((gb * t, D_ATT), lambda j: (j, 0)),
        out_shape=jax.ShapeDtypeStruct((batch * t, D_ATT), BF16),
        compiler_params=_cparams(("arbitrary",)),
        name="band_s",
    )(q, kt_hist, vt_hist, k_new, v_new, bm)


def _post_kernel(x_ref, of_ref, ol_ref, ob_ref, hg_ref, g_ref, wout_ref, wup_ref, fcw_ref, fcb_ref,
                 wdown_ref, y_ref, gt_ref, gcar_ref, *, tm, seg):
    multi = seg < tm
    groups = tm // SUBLANES
    gb = g_ref[1:2]
    gc = g_ref[2:3]
    gd = g_ref[3:4]
    mix = _dot(of_ref[...], wout_ref[0:D_ATT])
    mix = mix + _dot(ol_ref[...], wout_ref[D_ATT:D_ATT + D_LRU])
    mix = mix + _dot(ob_ref[...], wout_ref[D_ATT + D_LRU:])
    x1 = x_ref[...] + _rms(mix, gb)
    h2 = _rms(x1, gc).astype(BF16)

    if multi:
        rmod = lax.broadcasted_iota(jnp.int32, (tm, FF_CHUNK), 0) & (seg - 1)
    else:
        @pl.when(pl.program_id(1) == 0)
        def _():
            gcar_ref[...] = hg_ref[...]

    def up(c):
        return (_dot(h2, wup_ref[:, c * FF_CHUNK:(c + 1) * FF_CHUNK]),
                _dot(h2, wup_ref[:, D_FF + c * FF_CHUNK:D_FF + (c + 1) * FF_CHUNK]))

    n_chunks = D_FF // FF_CHUNK
    acc = jnp.zeros((tm, D_MODEL), F32)
    nxt = up(0)
    for c in range(n_chunks):
        cs = slice(c * FF_CHUNK, (c + 1) * FF_CHUNK)
        g, v = nxt
        if c + 1 < n_chunks:
            nxt = up(c + 1)
        if multi:
            prev1 = jnp.where(rmod >= 1, pltpu.roll(g, 1, 0), hg_ref[1, :, cs])
            prev2 = jnp.where(rmod >= 2, pltpu.roll(g, 2, 0), hg_ref[0, :, cs])
            gt_ref[:, cs] = g
        else:
            g3 = g.reshape(groups, SUBLANES, FF_CHUNK)
            tail = gcar_ref[:, cs]
            prev1 = _shift_rows(g3, tail, 1).reshape(tm, FF_CHUNK)
            prev2 = _shift_rows(g3, tail, 2).reshape(tm, FF_CHUNK)
            gt_ref[:, cs] = g[tm - SUBLANES:]
            gcar_ref[:, cs] = g[tm - SUBLANES:]
        gconv = prev2 * fcw_ref[0:1, cs]
        gconv = gconv + prev1 * fcw_ref[1:2, cs]
        gconv = gconv + g * fcw_ref[2:3, cs]
        gconv = gconv + fcb_ref[:, cs]
        act = (_gelu(gconv) * v).astype(BF16)
        acc = acc + _dot(act, wdown_ref[cs, :])
    y_ref[...] = x1 + _rms(acc, gd)


def _post_call(x, of, ol, ob, hg, p, w, layer, batch, seq, tm):
    n = batch * seq
    multi = seq < tm
    full = lambda a: pl.BlockSpec(a.shape, lambda *_: (0,) * a.ndim, pipeline_mode=pl.Buffered(1))
    layer_of = lambda a: pl.BlockSpec((None,) + a.shape[1:], lambda *_: (layer, 0, 0), pipeline_mode=pl.Buffered(1))
    if multi:
        grid = (1, 1)
        rows = lambda width: pl.BlockSpec((tm, width), lambda b, t: (0, 0))
        hg_spec = pl.BlockSpec(hg.shape, lambda b, t: (0, 0, 0))
        gt_shape = jax.ShapeDtypeStruct((n, D_FF), F32)
        gt_spec = rows(D_FF)
    else:
        nt = seq // tm
        grid = (batch, nt)
        rows = lambda width: pl.BlockSpec((tm, width), lambda b, t: (b * nt + t, 0))
        hg_spec = pl.BlockSpec((None, SUBLANES, D_FF), lambda b, t: (b, 0, 0))
        gt_shape = jax.ShapeDtypeStruct((batch, SUBLANES, D_FF), F32)
        gt_spec = pl.BlockSpec((None, SUBLANES, D_FF), lambda b, t: (b, 0, 0))
    return pl.pallas_call(
        functools.partial(_post_kernel, tm=tm, seg=min(seq, tm)),
        grid=grid,
        in_specs=[rows(D_MODEL), rows(D_ATT), rows(D_LRU), rows(D_ATT), hg_spec, full(p["g"]),
                  layer_of(w["wout"]), layer_of(w["wup"]), full(p["fcw"]), full(p["fcb"]), layer_of(w["wdown"])],
        out_specs=(rows(D_MODEL), gt_spec),
        out_shape=(jax.ShapeDtypeStruct((n, D_MODEL), F32), gt_shape),
        scratch_shapes=[pltpu.VMEM((SUBLANES, D_FF), F32)],
        compiler_params=_cparams(("arbitrary", "arbitrary")),
        name="post",
    )(x, of, ol, ob, hg, p["g"], w["wout"], w["wup"], p["fcw"], p["fcb"], w["wdown"])


def _block_diag_halves(w):
    nb, bw, _ = w.shape
    hb = nb // 2
    eye = jnp.eye(hb, dtype=w.dtype)
    w = w.reshape(2, hb, bw, bw)
    return (eye[None, :, None, :, None] * w[:, :, :, None, :]).reshape(2, hb * bw, hb * bw)


def _prep_layer(norm_g, w_in_t, b_forget, lru_conv_w, lru_conv_b, lru_wa, lru_ba, lru_wx, lru_bx,
                lru_lambda, rel_bias, ffn_conv_w, ffn_conv_b):
    o = 0
    wq, wk, wv = (w_in_t[o + j * D_ATT:o + (j + 1) * D_ATT] for j in range(3)); o += 3 * D_ATT
    wf = w_in_t[o:o + N_HEADS]; o += N_HEADS
    wlru = w_in_t[o:o + 2 * D_LRU]; o += 2 * D_LRU
    wqb, wkb, wvb = (w_in_t[o + j * D_ATT:o + (j + 1) * D_ATT] for j in range(3))
    wf = jnp.pad(wf, ((0, SUBLANES - N_HEADS), (0, 0)))
    return dict(
        g=norm_g,
        wq=wq, wk=wk, wv=wv, wf=wf,
        wfm=jnp.concatenate([wq, wk, wv, wkb, wvb, wf], axis=0),
        bf=jnp.broadcast_to(jnp.pad(b_forget, (0, SUBLANES - N_HEADS))[:, None], (SUBLANES, LANES)),
        wlru=wlru,
        wqb=wqb, wkb=wkb, wvb=wvb,
        cw=lru_conv_w, cb=lru_conv_b.reshape(1, D_LRU),
        wa=_block_diag_halves(lru_wa).astype(BF16), ba=lru_ba.reshape(1, D_LRU),
        wx=_block_diag_halves(lru_wx).astype(BF16), bx=lru_bx.reshape(1, D_LRU),
        lam=lru_lambda.reshape(1, D_LRU),
        rel=rel_bias,
        fcw=ffn_conv_w, fcb=ffn_conv_b.reshape(1, D_FF),
    )


def _prompt_layer(x, p, w, bm, layer, batch, seq):
    zeros8 = jnp.zeros((batch, SUBLANES, D_LRU), F32)
    (qf, kft, vft, kfa, vfa, lft, o_lru, hl, xt, qb, kba, vba, kbt, vbt) = _pre_call(x, zeros8, zeros8, p, batch, seq)
    o_fox = _fox_call(qf, kfa, vfa, batch, seq)
    o_band = _band_call(qb, kba, vba, bm, batch, seq)
    y, gt = _post_call(x, o_fox, o_lru, o_band, jnp.zeros((batch, SUBLANES, D_FF), F32), p, w, layer, batch, seq,
                       TILE)
    state = (kft, vft, lft, kbt, vbt, hl[:, SUBLANES - 1], xt[:, SUBLANES - (LRU_CONV - 1):],
             gt[:, SUBLANES - (FFN_CONV - 1):])
    return y, state


def _sample_layer(x, p, w, bm, layer, batch, t, fox_kt, fox_vt, fox_lf, band_kt, band_vt, lru_h0, lru_conv_h,
                  ffn_conv_h):
    n = batch * t
    (qf, kf, vf, kfn, vfn, lft, xb, yb, qb, kb, vb, kbn, vbn) = _pre_s_call(x, p, batch, t)
    lf_new = jnp.pad(lft.reshape(SUBLANES, batch, t).transpose(1, 0, 2), ((0, 0), (0, 0), (0, LANES - t)))
    o_fox = _fox_s_call(qf, fox_kt, fox_vt, fox_lf, kfn, vfn, lf_new, layer, batch, t)

    hx = jnp.stack([jnp.pad(lru_conv_h[:, j:], ((0, 0), (0, t - (LRU_CONV - 1 - j)), (0, 0))).reshape(n, D_LRU)
                    for j in range(LRU_CONV - 1)])
    h0 = jnp.repeat(lru_h0, t, axis=0)
    o_lru, hl = _lru_s_call(xb, yb, hx, h0, p, t)
    o_band = _band_s_call(qb, band_kt, band_vt, kbn, vbn, bm, layer, batch, t)
    hg = jnp.stack([jnp.pad(ffn_conv_h[:, j:], ((0, 0), (0, t - (FFN_CONV - 1 - j)), (0, 0))).reshape(n, D_FF)
                    for j in range(FFN_CONV - 1)])
    y, gt = _post_call(x, o_fox, o_lru, o_band, hg, p, w, layer, batch, t, n)

    state = (
        kf.reshape(batch, t, N_HEADS, HEAD_DIM),
        vf.reshape(batch, t, N_HEADS, HEAD_DIM),
        lft.reshape(SUBLANES, batch, t)[:N_HEADS].transpose(1, 2, 0),
        kb.reshape(batch, t, N_HEADS, HEAD_DIM),
        vb.reshape(batch, t, N_HEADS, HEAD_DIM),
        hl.reshape(batch, t, D_LRU)[:, t - 1],
        xb.reshape(batch, t, D_LRU)[:, t - (LRU_CONV - 1):],
        gt.reshape(batch, t, D_FF)[:, t - (FFN_CONV - 1):],
    )
    return y, state


def _feature_major(c):
    d, b, s = c.shape[:3]
    return c.transpose(0, 1, 3, 4, 2).reshape(d, b, D_ATT, s)


def _token_major(c):
    d, b, _, s = c.shape
    return c.reshape(d, b, N_HEADS, HEAD_DIM, s).transpose(0, 1, 4, 2, 3)


def kernel(x_prompt, x_sample, cache_fox_k, cache_fox_v, cache_fox_logf, cache_band_k, cache_band_v, state_lru_h, state_lru_conv, state_ffn_conv, norm_g, w_in, b_forget, lru_conv_w, lru_conv_b, lru_wa, lru_ba, lru_wx, lru_bx, lru_lambda, rel_bias, w_out, w_up, ffn_conv_w, ffn_conv_b, w_down):
    bp, seq, _ = x_prompt.shape
    bs, t, _ = x_sample.shape
    depth = norm_g.shape[0]
    assert seq % TILE == 0 and t < LANES and (t & (t - 1)) == 0 and cache_band_k.shape[2] == BAND_KEEP
    assert cache_fox_k.shape[2] % LANES == 0 and bs % SAMPLE_GROUP == 0
    xp = x_prompt.reshape(bp * seq, D_MODEL)
    xs = x_sample.reshape(bs * t, D_MODEL)
    w_in_t = w_in.transpose(0, 2, 1).astype(BF16)
    w = dict(wout=w_out.astype(BF16), wup=w_up.astype(BF16), wdown=w_down.astype(BF16))
    fox_kt, fox_vt = _feature_major(cache_fox_k), _feature_major(cache_fox_v)
    band_kt, band_vt = _feature_major(cache_band_k), _feature_major(cache_band_v)
    fox_lf = cache_fox_logf.transpose(0, 1, 3, 2)
    st_p, st_s = [], []
    for l in range(depth):
        p = _prep_layer(norm_g[l], w_in_t[l], b_forget[l], lru_conv_w[l], lru_conv_b[l], lru_wa[l], lru_ba[l],
                        lru_wx[l], lru_bx[l], lru_lambda[l], rel_bias[l], ffn_conv_w[l], ffn_conv_b[l])
        bm = _bias_call(p["rel"])
        xp, new_p = _prompt_layer(xp, p, w, bm, l, bp, seq)
        xs, new_s = _sample_layer(xs, p, w, bm, l, bs, t, fox_kt, fox_vt, fox_lf, band_kt, band_vt,
                                  state_lru_h[l], state_lru_conv[l], state_ffn_conv[l])
        st_p.append(new_p)
        st_s.append(new_s)

    stack = lambda states, j: jnp.stack([st[j] for st in states], axis=0)
    outs = [xp.reshape(bp, seq, D_MODEL), xs.reshape(bs, t, D_MODEL)]
    for j in range(8):
        sp, ss = stack(st_p, j), stack(st_s, j)
        if j in (0, 1, 3, 4):
            sp = _token_major(sp)
        elif j == 2:
            sp = sp[:, :, :N_HEADS].transpose(0, 1, 3, 2)
        outs += [sp, ss]
    return tuple(outs)
```

```python
import functools

import jax
import jax.numpy as jnp
from jax import lax
from jax.experimental import pallas as pl
from jax.experimental.pallas import tpu as pltpu

D_MODEL = 1024
CHUNK = 64
HEAD_DIM = 64
N_HEADS = 4
D_ATT = N_HEADS * HEAD_DIM
D_LRU = 512
LRU_HALF = D_LRU // 2
LRU_CONV = 4
RGLRU_C = 8.0
BAND_PREV = 8
BAND_KEEP = BAND_PREV * CHUNK
REL_CLIP = 128
N_REL = REL_CLIP + CHUNK
D_FF = 3 * D_MODEL
FF_CHUNK = 1536
FFN_CONV = 3
RMS_EPS = 1e-6
NEG_INF = -1e30
ATTN_SCALE = HEAD_DIM ** -0.5
LOG2E = 1.4426950408889634

LANES = 128
SUBLANES = 8
HEAD_PAD = 80
KV_ROWS = N_HEADS * HEAD_PAD
Q_COLS = N_HEADS * LANES
N_CPARTS = 3
TILE = 512
SAMPLE_GROUP = 4
BAND_TQ = 4 * CHUNK
BAND_TK = BAND_TQ + BAND_KEEP
BAND_TILES = 2
VMEM_LIMIT = 56 * 1024 * 1024

BF16 = jnp.bfloat16
F32 = jnp.float32


def _cparams(sem):
    return pltpu.CompilerParams(dimension_semantics=sem, vmem_limit_bytes=VMEM_LIMIT)


def _rms(x, g):
    y = x * lax.rsqrt(jnp.mean(x * x, axis=-1, keepdims=True) + RMS_EPS)
    return y * g


def _log_sigmoid(x):
    return jnp.minimum(x, 0.0) - jnp.log1p(jnp.exp(-jnp.abs(x)))


def _gelu(x):
    return jax.nn.gelu(x)


def _dot(a, b):
    return jnp.dot(a, b, preferred_element_type=F32)


def _dot_nt(a, b):
    return lax.dot_general(a, b, (((1,), (1,)), ((), ())), preferred_element_type=F32)


def _cumsum_lanes(x):
    n = x.shape[-1]
    lane = lax.broadcasted_iota(jnp.int32, x.shape, x.ndim - 1)
    s = 1
    while s < n:
        x = x + jnp.where(lane >= s, pltpu.roll(x, s, x.ndim - 1), 0.0)
        s *= 2
    return x


def _rows(vals, n, width):
    r = lax.broadcasted_iota(jnp.int32, (n, width), 0)
    out = jnp.zeros((n, width), F32)
    for j, v in enumerate(vals):
        out = jnp.where(r == j, v, out)
    return out


def _shift_rows(x3, tail, d):
    r = pltpu.roll(x3, d, 1)
    prev = jnp.concatenate([pltpu.roll(tail[None], d, 1), r[:-1]], axis=0)
    sub = lax.broadcasted_iota(jnp.int32, x3.shape, 1)
    return jnp.where(sub >= d, r, prev)


def _gate_logits(xc, wa_ref, wx_ref):
    xcb = xc.astype(BF16)

    def block_diag(w_ref):
        return jnp.concatenate([_dot(xcb[:, :LRU_HALF], w_ref[0]), _dot(xcb[:, LRU_HALF:], w_ref[1])], axis=1)

    return block_diag(wa_ref), block_diag(wx_ref)


def _gate_terms(logits, xc, ba, bx, lam):
    r = jax.nn.sigmoid(logits[0] + ba)
    ig = jax.nn.sigmoid(logits[1] + bx)
    log_a = RGLRU_C * r * _log_sigmoid(lam)
    a = jnp.exp(log_a)
    th = jnp.tanh(log_a)
    u = (jnp.sqrt(-2.0 * th) * lax.rsqrt(1.0 - th)) * (ig * xc)
    return a, u


def _pre_kernel(x_ref, g_ref, wfm_ref, bf_ref, wlru_ref,
                hx_ref, h0_ref, cw_ref, cb_ref, wa_ref, ba_ref, wx_ref, bx_ref, lam_ref,
                qfa_ref, kft_ref, vft_ref, kfa_ref, vfa_ref, lft_ref, ol_ref, hl_ref, xt_ref,
                qba_ref, kba_ref, vba_ref, kbt_ref, vbt_ref, ccar_ref, xtail_ref, hcar_ref, *, tm, nt):
    t = pl.program_id(1)
    helper = HEAD_PAD - HEAD_DIM
    groups = tm // SUBLANES

    @pl.when(t == 0)
    def _():
        ccar_ref[...] = jnp.zeros_like(ccar_ref)
        xtail_ref[...] = hx_ref[...]
        hcar_ref[...] = h0_ref[...]
        lane = lax.broadcasted_iota(jnp.int32, kba_ref.shape, 1) & (LANES - 1)
        kba_ref[...] = jnp.where(lane == HEAD_DIM, NEG_INF, 0.0).astype(BF16)
        vba_ref[...] = jnp.zeros_like(vba_ref)

    @pl.when(t > 0)
    def _():
        h = _rms(x_ref[...], g_ref[...]).astype(BF16)
        lru = _dot_nt(h, wlru_ref[...])
        xb = lru[:, :D_LRU]
        yb = lru[:, D_LRU:]

        x3 = xb.reshape(groups, SUBLANES, D_LRU)
        tail = xtail_ref[...]
        xc = _shift_rows(x3, tail, 3) * cw_ref[0:1]
        xc = xc + _shift_rows(x3, tail, 2) * cw_ref[1:2]
        xc = xc + _shift_rows(x3, tail, 1) * cw_ref[2:3]
        xc = xc + x3 * cw_ref[3:4]
        xc = (xc + cb_ref[...]).reshape(tm, D_LRU)

        fm = _dot_nt(wfm_ref[...], h)
        qt, kt, vt, qbt, kbt, vbt = (fm[j * D_ATT:(j + 1) * D_ATT] for j in range(6))
        lf = _log_sigmoid(fm[6 * D_ATT:] + bf_ref[:, 0:1])
        lft_ref[...] = lf
        c = _cumsum_lanes(lf) + ccar_ref[:, 0:1]
        ccar_ref[...] = jnp.broadcast_to(c[:, tm - 1:tm], ccar_ref.shape)
        c = c * LOG2E
        c1 = c.astype(BF16)
        r1 = c - c1.astype(F32)
        c2 = r1.astype(BF16)
        c3 = (r1 - c2.astype(F32)).astype(BF16)
        ones_row = _rows([jnp.ones((1, tm), F32)], helper, tm).astype(BF16)

        kft_ref[...] = kt
        vft_ref[...] = vt
        ones_q = _rows([jnp.ones((1, tm), F32)] * N_CPARTS, helper, tm).astype(BF16)
        k_tiles = []
        for hh in range(N_HEADS):
            r0 = hh * HEAD_PAD
            f0 = hh * HEAD_DIM
            qfa_ref[r0:r0 + HEAD_DIM, :] = (qt[f0:f0 + HEAD_DIM] * (ATTN_SCALE * LOG2E)).astype(BF16)
            qfa_ref[r0 + HEAD_DIM:r0 + HEAD_PAD, :] = ones_q
            neg_c = [-(p[hh:hh + 1].astype(F32)) for p in (c1, c2, c3)]
            tile = jnp.concatenate([kt[f0:f0 + HEAD_DIM], _rows(neg_c, LANES - HEAD_DIM, tm)], axis=0)
            k_tiles.append(tile.T.astype(BF16))
            vfa_ref[r0:r0 + HEAD_DIM, :] = vt[f0:f0 + HEAD_DIM].astype(BF16)
            vfa_ref[r0 + HEAD_DIM:r0 + HEAD_PAD, :] = ones_row
        kfa_ref[...] = jnp.concatenate(k_tiles, axis=1)
        ones_qb = _rows([jnp.ones((1, tm), F32)], helper, tm).astype(BF16)
        k_tiles = []
        for hh in range(N_HEADS):
            r0 = hh * HEAD_PAD
            f0 = hh * HEAD_DIM
            qba_ref[r0:r0 + HEAD_DIM, :] = (qbt[f0:f0 + HEAD_DIM] * (ATTN_SCALE * LOG2E)).astype(BF16)
            qba_ref[r0 + HEAD_DIM:r0 + HEAD_PAD, :] = ones_qb
            tile = jnp.concatenate([kbt[f0:f0 + HEAD_DIM], jnp.zeros((LANES - HEAD_DIM, tm), F32)], axis=0)
            k_tiles.append(tile.T.astype(BF16))
            vba_ref[r0:r0 + HEAD_DIM, :] = vbt[f0:f0 + HEAD_DIM].astype(BF16)
            vba_ref[r0 + HEAD_DIM:r0 + HEAD_PAD, :] = ones_row
        kba_ref[...] = jnp.concatenate(k_tiles, axis=1)

        logits = _gate_logits(xc, wa_ref, wx_ref)
        a, u = _gate_terms(logits, xc, ba_ref[...], bx_ref[...], lam_ref[...])
        a = a.reshape(groups, SUBLANES, D_LRU)
        u = u.reshape(groups, SUBLANES, D_LRU)
        sub = lax.broadcasted_iota(jnp.int32, a.shape, 1)
        s = 1
        while s < SUBLANES:
            ok = sub >= s
            u = jnp.where(ok, a * pltpu.roll(u, s, 1) + u, u)
            a = jnp.where(ok, a * pltpu.roll(a, s, 1), a)
            s *= 2
        hprev = hcar_ref[SUBLANES - 1:SUBLANES]
        hs = []
        for gi in range(groups):
            hg = a[gi] * hprev + u[gi]
            hs.append(hg)
            hprev = hg[SUBLANES - 1:SUBLANES]
        hseq = jnp.concatenate(hs, axis=0)
        ol_ref[...] = (hseq * _gelu(yb)).astype(BF16)
        hl_ref[...] = hs[-1]
        hcar_ref[...] = hs[-1]
        xt_ref[...] = xb[tm - SUBLANES:]
        xtail_ref[...] = xb[tm - SUBLANES:]

        @pl.when(t == nt)
        def _():
            kbt_ref[...] = kbt
            vbt_ref[...] = vbt


def _pre_call(x, hx, h0, p, batch, seq):
    tm = TILE
    assert tm == BAND_KEEP
    nt = seq // tm
    n = batch * seq
    tok = lambda w: pl.BlockSpec((tm, w), lambda b, t: (b * nt + jnp.maximum(t - 1, 0), 0))
    feat = lambda r: pl.BlockSpec((None, r, tm), lambda b, t: (b, 0, jnp.maximum(t - 1, 0)))
    padded = pl.BlockSpec((None, KV_ROWS, tm), lambda b, t: (b, 0, t))
    padded_tok = pl.BlockSpec((None, tm, Q_COLS), lambda b, t: (b, t, 0))
    last = pl.BlockSpec((None, D_ATT, tm), lambda b, t: (b, 0, 0))
    state = pl.BlockSpec((None, SUBLANES, D_LRU), lambda b, t: (b, 0, 0))
    full = lambda a: pl.BlockSpec(a.shape, lambda b, t: (0,) * a.ndim)
    sd = jax.ShapeDtypeStruct
    ws1 = [p["g"][0:1], p["wfm"], p["bf"], p["wlru"]]
    ws2 = [p["cw"], p["cb"], p["wa"], p["ba"], p["wx"], p["bx"], p["lam"]]
    st = sd((batch, SUBLANES, D_LRU), F32)
    out_shape = (
        sd((batch, KV_ROWS, seq), BF16), sd((batch, D_ATT, seq), F32), sd((batch, D_ATT, seq), F32),
        sd((n, Q_COLS), BF16), sd((batch, KV_ROWS, seq), BF16),
        sd((batch, SUBLANES, seq), F32), sd((n, D_LRU), BF16), st, st,
        sd((batch, KV_ROWS, seq), BF16), sd((batch, seq + tm, Q_COLS), BF16),
        sd((batch, KV_ROWS, seq + tm), BF16),
        sd((batch, D_ATT, tm), F32), sd((batch, D_ATT, tm), F32),
    )
    out_specs = (
        feat(KV_ROWS), feat(D_ATT), feat(D_ATT), tok(Q_COLS), feat(KV_ROWS),
        feat(SUBLANES), tok(D_LRU), state, state,
        feat(KV_ROWS), padded_tok, padded, last, last,
    )
    return pl.pallas_call(
        functools.partial(_pre_kernel, tm=tm, nt=nt),
        grid=(batch, nt + 1),
        in_specs=[tok(D_MODEL)] + [full(w) for w in ws1] + [state, state] + [full(w) for w in ws2],
        out_specs=out_specs,
        out_shape=out_shape,
        scratch_shapes=[pltpu.VMEM((SUBLANES, LANES), F32), pltpu.VMEM((SUBLANES, D_LRU), F32),
                        pltpu.VMEM((SUBLANES, D_LRU), F32)],
        compiler_params=_cparams(("arbitrary", "arbitrary")),
        name="pre",
    )(x, *ws1, hx, h0, *ws2)


def _pre_s_kernel(x_ref, g_ref, wq_ref, wk_ref, wv_ref, wf_ref, bf_ref, wlru_ref, wqb_ref, wkb_ref, wvb_ref,
                  qf_ref, kf_ref, vf_ref, kfn_ref, vfn_ref, lft_ref, xb_ref, yb_ref,
                  qb_ref, kb_ref, vb_ref, kbn_ref, vbn_ref, *, batch, t):
    h = _rms(x_ref[...], g_ref[...]).astype(BF16)
    qf_ref[...] = (_dot_nt(h, wq_ref[...]) * ATTN_SCALE).astype(BF16)
    qb_ref[...] = (_dot_nt(h, wqb_ref[...]) * ATTN_SCALE).astype(BF16)
    lft_ref[...] = _log_sigmoid(_dot_nt(wf_ref[...], h) + bf_ref[:, 0:1])
    lru = _dot_nt(h, wlru_ref[...])
    xb_ref[...] = lru[:, :D_LRU]
    yb_ref[...] = lru[:, D_LRU:]
    for w_ref, o_ref, on_ref in ((wk_ref, kf_ref, kfn_ref), (wv_ref, vf_ref, vfn_ref),
                                 (wkb_ref, kb_ref, kbn_ref), (wvb_ref, vb_ref, vbn_ref)):
        y = _dot_nt(h, w_ref[...])
        o_ref[...] = y
        on_ref[...] = jnp.zeros_like(on_ref)
        yb16 = y.astype(BF16)
        for b in range(batch):
            on_ref[b, 0:t, :] = yb16[b * t:(b + 1) * t]


def _pre_s_call(x, p, batch, t):
    n = batch * t
    sd = jax.ShapeDtypeStruct
    ws = [p["g"][0:1], p["wq"], p["wk"], p["wv"], p["wf"], p["bf"], p["wlru"], p["wqb"], p["wkb"], p["wvb"]]
    new = sd((batch, LANES, D_ATT), BF16)
    tokm = sd((n, D_ATT), F32)
    out_shape = (
        sd((n, D_ATT), BF16), tokm, tokm, new, new, sd((SUBLANES, n), F32),
        sd((n, D_LRU), F32), sd((n, D_LRU), F32),
        sd((n, D_ATT), BF16), tokm, tokm, new, new,
    )
    return pl.pallas_call(
        functools.partial(_pre_s_kernel, batch=batch, t=t),
        out_shape=out_shape,
        compiler_params=pltpu.CompilerParams(vmem_limit_bytes=VMEM_LIMIT),
        name="pre_s",
    )(x, *ws)


def _fox_kernel(q_ref, k_ref, v_ref, o_ref, sa_ref, sb_ref, *, tb):
    i = pl.program_id(1)
    row = lax.broadcasted_iota(jnp.int32, (tb, tb), 0)
    col = lax.broadcasted_iota(jnp.int32, (tb, tb), 1)
    causal = row <= col
    heads = range(N_HEADS)
    q = [q_ref[h * HEAD_PAD:(h + 1) * HEAD_PAD, :] for h in heads]

    def scores(kb, s_ref):
        start = pl.multiple_of(kb * tb, tb)
        for h in heads:
            s_ref[h] = _dot(k_ref[pl.ds(start, tb), h * LANES:h * LANES + HEAD_PAD], q[h])

    def block(kb, s_ref, carry, masked):
        start = pl.multiple_of(kb * tb, tb)
        ps, alphas, ms = [], [], []
        for h in heads:
            sh = s_ref[h]
            if masked:
                sh = jnp.where(causal, sh, NEG_INF)
            m = carry[2 * h]
            m_new = jnp.maximum(m, jnp.max(sh, axis=0, keepdims=True))
            alphas.append(jnp.exp2(m - m_new))
            ps.append(jnp.exp2(sh - m_new).astype(BF16))
            ms.append(m_new)
        out = []
        for h in heads:
            pv = _dot(v_ref[h * HEAD_PAD:(h + 1) * HEAD_PAD, pl.ds(start, tb)], ps[h])
            out += [ms[h], alphas[h] * carry[2 * h + 1] + pv]
        return tuple(out)

    init = []
    for h in heads:
        init += [jnp.full((1, tb), NEG_INF, F32), jnp.zeros((HEAD_PAD, tb), F32)]
    scores(0, sa_ref)

    def pair(j, carry):
        kb = 2 * j
        scores(kb + 1, sb_ref)
        carry = block(kb, sa_ref, carry, False)
        scores(kb + 2, sa_ref)
        return block(kb + 1, sb_ref, carry, False)

    carry = lax.fori_loop(0, i // 2, pair, tuple(init))

    def odd(carry):
        scores(i, sb_ref)
        carry = block(i - 1, sa_ref, carry, False)
        return block(i, sb_ref, carry, True)

    def even(carry):
        return block(i, sa_ref, carry, True)

    carry = lax.cond((i & 1) == 1, odd, even, carry)
    outs = []
    for h in heads:
        acc = carry[2 * h + 1]
        outs.append(acc[:HEAD_DIM] / acc[HEAD_DIM:HEAD_DIM + 1])
    o_ref[...] = jnp.concatenate(outs, axis=0).T.astype(BF16)


def _fox_call(q, k, v, batch, seq):
    tb = TILE
    nq = seq // tb
    return pl.pallas_call(
        functools.partial(_fox_kernel, tb=tb),
        grid=(batch, nq),
        in_specs=[
            pl.BlockSpec((None, KV_ROWS, tb), lambda b, i: (b, 0, i)),
            pl.BlockSpec((seq, Q_COLS), lambda b, i: (b, 0)),
            pl.BlockSpec((None, KV_ROWS, seq), lambda b, i: (b, 0, 0)),
        ],
        out_specs=pl.BlockSpec((tb, D_ATT), lambda b, i: (b * nq + i, 0)),
        out_shape=jax.ShapeDtypeStruct((batch * seq, D_ATT), BF16),
        scratch_shapes=[pltpu.VMEM((N_HEADS, tb, tb), F32), pltpu.VMEM((N_HEADS, tb, tb), F32)],
        compiler_params=_cparams(("arbitrary", "arbitrary")),
        name="fox",
    )(q, k, v)


def _fox_s_kernel(q_ref, kh_ref, vh_ref, lfh_ref, kn_ref, vn_ref, lfn_ref, o_ref, *, t, gb):
    row = lax.broadcasted_iota(jnp.int32, (t, LANES), 0)
    col = lax.broadcasted_iota(jnp.int32, (t, LANES), 1)
    causal = col <= row
    for g in range(gb):
        rs = slice(g * t, (g + 1) * t)
        c_h = _cumsum_lanes(lfh_ref[g])
        past = c_h.shape[-1]
        c_n = c_h[:, past - 1:past] + _cumsum_lanes(lfn_ref[g, 0:N_HEADS, :])
        for h in range(N_HEADS):
            fs = slice(h * HEAD_DIM, (h + 1) * HEAD_DIM)
            q = q_ref[rs, fs]
            s_h = _dot(q, kh_ref[g, fs, :].astype(BF16)) - c_h[h:h + 1]
            s_n = _dot_nt(q, kn_ref[g, :, fs]) - c_n[h:h + 1]
            s_n = jnp.where(causal, s_n, NEG_INF)
            m = jnp.maximum(jnp.max(s_h, axis=-1, keepdims=True), jnp.max(s_n, axis=-1, keepdims=True))
            p_h = jnp.exp(s_h - m)
            p_n = jnp.exp(s_n - m)
            l = jnp.sum(p_h, axis=-1, keepdims=True) + jnp.sum(p_n, axis=-1, keepdims=True)
            o = (_dot_nt(p_h.astype(BF16), vh_ref[g, fs, :].astype(BF16))
                 + _dot(p_n.astype(BF16), vn_ref[g, :, fs]))
            o_ref[rs, fs] = (o / l).astype(BF16)


def _fox_s_call(q, kt_hist, vt_hist, lf_hist, k_new, v_new, lf_new, layer, batch, t):
    past = kt_hist.shape[-1]
    gb = SAMPLE_GROUP
    hist = pl.BlockSpec((None, gb, D_ATT, past), lambda j: (layer, j, 0, 0))
    new = pl.BlockSpec((gb, LANES, D_ATT), lambda j: (j, 0, 0))
    return pl.pallas_call(
        functools.partial(_fox_s_kernel, t=t, gb=gb),
        grid=(batch // gb,),
        in_specs=[
            pl.BlockSpec((gb * t, D_ATT), lambda j: (j, 0)),
            hist, hist,
            pl.BlockSpec((None, gb, N_HEADS, past), lambda j: (layer, j, 0, 0)),
            new, new,
            pl.BlockSpec((gb, SUBLANES, LANES), lambda j: (j, 0, 0)),
        ],
        out_specs=pl.BlockSpec((gb * t, D_ATT), lambda j: (j, 0)),
        out_shape=jax.ShapeDtypeStruct((batch * t, D_ATT), BF16),
        compiler_params=_cparams(("arbitrary",)),
        name="fox_s",
    )(q, kt_hist, vt_hist, lf_hist, k_new, v_new, lf_new)


def _lru_s_kernel(xb_ref, yb_ref, hx_ref, h0_ref, cw_ref, cb_ref, wa_ref, ba_ref, wx_ref, bx_ref,
                  lam_ref, o_ref, hl_ref, *, seg):
    x = xb_ref[...]
    rmod = lax.broadcasted_iota(jnp.int32, x.shape, 0) & (seg - 1)
    prevs = [jnp.where(rmod >= d, pltpu.roll(x, d, 0), hx_ref[LRU_CONV - 1 - d]) for d in range(1, LRU_CONV)]
    xc = prevs[2] * cw_ref[0:1]
    xc = xc + prevs[1] * cw_ref[1:2]
    xc = xc + prevs[0] * cw_ref[2:3]
    xc = xc + x * cw_ref[3:4]
    xc = xc + cb_ref[...]
    a, u = _gate_terms(_gate_logits(xc, wa_ref, wx_ref), xc, ba_ref[...], bx_ref[...], lam_ref[...])
    s = 1
    while s < seg:
        ok = rmod >= s
        u = jnp.where(ok, a * pltpu.roll(u, s, 0) + u, u)
        a = jnp.where(ok, a * pltpu.roll(a, s, 0), a)
        s *= 2
    hseq = a * h0_ref[...] + u
    o_ref[...] = (hseq * _gelu(yb_ref[...])).astype(BF16)
    hl_ref[...] = hseq


def _lru_s_call(xb, yb, hx, h0, p, seg):
    n = xb.shape[0]
    return pl.pallas_call(
        functools.partial(_lru_s_kernel, seg=seg),
        out_shape=(jax.ShapeDtypeStruct((n, D_LRU), BF16), jax.ShapeDtypeStruct((n, D_LRU), F32)),
        compiler_params=pltpu.CompilerParams(vmem_limit_bytes=VMEM_LIMIT),
        name="lru_s",
    )(xb, yb, hx, h0, p["cw"], p["cb"], p["wa"], p["ba"], p["wx"], p["bx"], p["lam"])


def _bias_kernel(u_ref, o_ref, ot_ref):
    r = lax.broadcasted_iota(jnp.int32, (BAND_TQ, BAND_TK), 0)
    c = lax.broadcasted_iota(jnp.int32, (BAND_TQ, BAND_TK), 1)
    dch = (c >> 6) - (r >> 6)
    inband = (dch >= 0) & (dch <= BAND_PREV)
    for h in range(N_HEADS):
        ub = jnp.broadcast_to(u_ref[h:h + 1, :], (BAND_TQ, BAND_TK))
        bias = jnp.where(inband, pltpu.roll(ub, 0, 1, stride=1, stride_axis=0), NEG_INF)
        o_ref[h] = bias
        ot_ref[h] = (bias * LOG2E).T


def _bias_call(rel):
    left = BAND_KEEP - REL_CLIP
    u = jnp.concatenate([jnp.broadcast_to(rel[:, 0:1], (N_HEADS, left)), rel,
                         jnp.broadcast_to(rel[:, 0:1], (N_HEADS, BAND_TK - left - N_REL))], axis=1)
    return pl.pallas_call(
        _bias_kernel,
        out_shape=(jax.ShapeDtypeStruct((N_HEADS, BAND_TQ, BAND_TK), F32),
                   jax.ShapeDtypeStruct((N_HEADS, BAND_TK, BAND_TQ), F32)),
        compiler_params=pltpu.CompilerParams(vmem_limit_bytes=VMEM_LIMIT),
        name="band_bias",
    )(u)


def _band_kernel(q_ref, k_ref, v_ref, bm_ref, o_ref):
    i = pl.program_id(1)
    heads = range(N_HEADS)
    starts = [pl.multiple_of((i * BAND_TILES + j) * BAND_TQ, BAND_TQ) for j in range(BAND_TILES)]
    s = [[_dot(k_ref[pl.ds(starts[j], BAND_TK), h * LANES:h * LANES + HEAD_PAD],
               q_ref[h * HEAD_PAD:(h + 1) * HEAD_PAD, j * BAND_TQ:(j + 1) * BAND_TQ]) + bm_ref[h]
          for h in heads] for j in range(BAND_TILES)]
    for j in range(BAND_TILES):
        ps = []
        for h in heads:
            m = jnp.max(s[j][h], axis=0, keepdims=True)
            ps.append(jnp.exp2(s[j][h] - m).astype(BF16))
        outs = []
        for h in heads:
            ov = _dot(v_ref[h * HEAD_PAD:(h + 1) * HEAD_PAD, pl.ds(starts[j], BAND_TK)], ps[h])
            outs.append(ov[:HEAD_DIM] / ov[HEAD_DIM:HEAD_DIM + 1])
        o_ref[j * BAND_TQ:(j + 1) * BAND_TQ, :] = jnp.concatenate(outs, axis=0).T.astype(BF16)


def _band_call(q, kpad, vpad, bm, batch, seq):
    rows = BAND_TILES * BAND_TQ
    nq = seq // rows
    lk = seq + BAND_KEEP
    return pl.pallas_call(
        _band_kernel,
        grid=(batch, nq),
        in_specs=[
            pl.BlockSpec((None, KV_ROWS, rows), lambda b, i: (b, 0, i)),
            pl.BlockSpec((None, lk, Q_COLS), lambda b, i: (b, 0, 0)),
            pl.BlockSpec((None, KV_ROWS, lk), lambda b, i: (b, 0, 0)),
            pl.BlockSpec(bm.shape, lambda b, i: (0, 0, 0)),
        ],
        out_specs=pl.BlockSpec((rows, D_ATT), lambda b, i: (b * nq + i, 0)),
        out_shape=jax.ShapeDtypeStruct((batch * seq, D_ATT), BF16),
        compiler_params=_cparams(("arbitrary", "arbitrary")),
        name="band",
    )(q, kpad, vpad, bm)


def _band_s_kernel(q_ref, kh_ref, vh_ref, kn_ref, vn_ref, bm_ref, o_ref, *, t, gb):
    col = lax.broadcasted_iota(jnp.int32, (t, LANES), 1)
    exists = col < t
    for g in range(gb):
        rs = slice(g * t, (g + 1) * t)
        for h in range(N_HEADS):
            fs = slice(h * HEAD_DIM, (h + 1) * HEAD_DIM)
            q = q_ref[rs, fs]
            s_h = _dot(q, kh_ref[g, fs, :].astype(BF16)) + bm_ref[h, :t, :BAND_KEEP]
            s_n = _dot_nt(q, kn_ref[g, :, fs]) + bm_ref[h, :t, BAND_KEEP:BAND_KEEP + LANES]
            s_n = jnp.where(exists, s_n, NEG_INF)
            m = jnp.maximum(jnp.max(s_h, axis=-1, keepdims=True), jnp.max(s_n, axis=-1, keepdims=True))
            p_h = jnp.exp(s_h - m)
            p_n = jnp.exp(s_n - m)
            l = jnp.sum(p_h, axis=-1, keepdims=True) + jnp.sum(p_n, axis=-1, keepdims=True)
            o = (_dot_nt(p_h.astype(BF16), vh_ref[g, fs, :].astype(BF16))
                 + _dot(p_n.astype(BF16), vn_ref[g, :, fs]))
            o_ref[rs, fs] = (o / l).astype(BF16)


def _band_s_call(q, kt_hist, vt_hist, k_new, v_new, bm, layer, batch, t):
    gb = SAMPLE_GROUP
    hist = pl.BlockSpec((None, gb, D_ATT, BAND_KEEP), lambda j: (layer, j, 0, 0))
    new = pl.BlockSpec((gb, LANES, D_ATT), lambda j: (j, 0, 0))
    return pl.pallas_call(
        functools.partial(_band_s_kernel, t=t, gb=gb),
        grid=(batch // gb,),
        in_specs=[pl.BlockSpec((gb * t, D_ATT), lambda j: (j, 0)), hist, hist, new, new,
                  pl.BlockSpec(bm.shape, lambda j: (0, 0, 0))],
        out_specs=pl.BlockSpec((gb * t, D_ATT), lambda j: (j, 0)),
        out_shape=jax.ShapeDtypeStruct((batch * t, D_ATT), BF16),
        compiler_params=_cparams(("arbitrary",)),
        name="band_s",
    )(q, kt_hist, vt_hist, k_new, v_new, bm)


def _post_kernel(x_ref, of_ref, ol_ref, ob_ref, hg_ref, g_ref, wout_ref, wup_ref, fcw_ref, fcb_ref,
                 wdown_ref, y_ref, gt_ref, gcar_ref, *, tm, seg):
    multi = seg < tm
    groups = tm // SUBLANES
    gb = g_ref[1:2]
    gc = g_ref[2:3]
    gd = g_ref[3:4]
    mix = _dot(of_ref[...], wout_ref[0:D_ATT])
    mix = mix + _dot(ol_ref[...], wout_ref[D_ATT:D_ATT + D_LRU])
    mix = mix + _dot(ob_ref[...], wout_ref[D_ATT + D_LRU:])
    x1 = x_ref[...] + _rms(mix, gb)
    h2 = _rms(x1, gc).astype(BF16)

    if multi:
        rmod = lax.broadcasted_iota(jnp.int32, (tm, FF_CHUNK), 0) & (seg - 1)
    else:
        @pl.when(pl.program_id(1) == 0)
        def _():
            gcar_ref[...] = hg_ref[...]

    def up(c):
        return (_dot(h2, wup_ref[:, c * FF_CHUNK:(c + 1) * FF_CHUNK]),
                _dot(h2, wup_ref[:, D_FF + c * FF_CHUNK:D_FF + (c + 1) * FF_CHUNK]))

    n_chunks = D_FF // FF_CHUNK
    acc = jnp.zeros((tm, D_MODEL), F32)
    nxt = up(0)
    for c in range(n_chunks):
        cs = slice(c * FF_CHUNK, (c + 1) * FF_CHUNK)
        g, v = nxt
        if c + 1 < n_chunks:
            nxt = up(c + 1)
        if multi:
            prev1 = jnp.where(rmod >= 1, pltpu.roll(g, 1, 0), hg_ref[1, :, cs])
            prev2 = jnp.where(rmod >= 2, pltpu.roll(g, 2, 0), hg_ref[0, :, cs])
            gt_ref[:, cs] = g
        else:
            g3 = g.reshape(groups, SUBLANES, FF_CHUNK)
            tail = gcar_ref[:, cs]
            prev1 = _shift_rows(g3, tail, 1).reshape(tm, FF_CHUNK)
            prev2 = _shift_rows(g3, tail, 2).reshape(tm, FF_CHUNK)
            gt_ref[:, cs] = g[tm - SUBLANES:]
            gcar_ref[:, cs] = g[tm - SUBLANES:]
        gconv = prev2 * fcw_ref[0:1, cs]
        gconv = gconv + prev1 * fcw_ref[1:2, cs]
        gconv = gconv + g * fcw_ref[2:3, cs]
        gconv = gconv + fcb_ref[:, cs]
        act = (_gelu(gconv) * v).astype(BF16)
        acc = acc + _dot(act, wdown_ref[cs, :])
    y_ref[...] = x1 + _rms(acc, gd)


def _post_call(x, of, ol, ob, hg, p, w, layer, batch, seq, tm):
    n = batch * seq
    multi = seq < tm
    full = lambda a: pl.BlockSpec(a.shape, lambda *_: (0,) * a.ndim, pipeline_mode=pl.Buffered(1))
    layer_of = lambda a: pl.BlockSpec((None,) + a.shape[1:], lambda *_: (layer, 0, 0), pipeline_mode=pl.Buffered(1))
    if multi:
        grid = (1, 1)
        rows = lambda width: pl.BlockSpec((tm, width), lambda b, t: (0, 0))
        hg_spec = pl.BlockSpec(hg.shape, lambda b, t: (0, 0, 0))
        gt_shape = jax.ShapeDtypeStruct((n, D_FF), F32)
        gt_spec = rows(D_FF)
    else:
        nt = seq // tm
        grid = (batch, nt)
        rows = lambda width: pl.BlockSpec((tm, width), lambda b, t: (b * nt + t, 0))
        hg_spec = pl.BlockSpec((None, SUBLANES, D_FF), lambda b, t: (b, 0, 0))
        gt_shape = jax.ShapeDtypeStruct((batch, SUBLANES, D_FF), F32)
        gt_spec = pl.BlockSpec((None, SUBLANES, D_FF), lambda b, t: (b, 0, 0))
    return pl.pallas_call(
        functools.partial(_post_kernel, tm=tm, seg=min(seq, tm)),
        grid=grid,
        in_specs=[rows(D_MODEL), rows(D_ATT), rows(D_LRU), rows(D_ATT), hg_spec, full(p["g"]),
                  layer_of(w["wout"]), layer_of(w["wup"]), full(p["fcw"]), full(p["fcb"]), layer_of(w["wdown"])],
        out_specs=(rows(D_MODEL), gt_spec),
        out_shape=(jax.ShapeDtypeStruct((n, D_MODEL), F32), gt_shape),
        scratch_shapes=[pltpu.VMEM((SUBLANES, D_FF), F32)],
        compiler_params=_cparams(("arbitrary", "arbitrary")),
        name="post",
    )(x, of, ol, ob, hg, p["g"], w["wout"], w["wup"], p["fcw"], p["fcb"], w["wdown"])


def _block_diag_halves(w):
    nb, bw, _ = w.shape
    hb = nb // 2
    eye = jnp.eye(hb, dtype=w.dtype)
    w = w.reshape(2, hb, bw, bw)
    return (eye[None, :, None, :, None] * w[:, :, :, None, :]).reshape(2, hb * bw, hb * bw)


def _prep_layer(norm_g, w_in_t, b_forget, lru_conv_w, lru_conv_b, lru_wa, lru_ba, lru_wx, lru_bx,
                lru_lambda, rel_bias, ffn_conv_w, ffn_conv_b):
    o = 0
    wq, wk, wv = (w_in_t[o + j * D_ATT:o + (j + 1) * D_ATT] for j in range(3)); o += 3 * D_ATT
    wf = w_in_t[o:o + N_HEADS]; o += N_HEADS
    wlru = w_in_t[o:o + 2 * D_LRU]; o += 2 * D_LRU
    wqb, wkb, wvb = (w_in_t[o + j * D_ATT:o + (j + 1) * D_ATT] for j in range(3))
    wf = jnp.pad(wf, ((0, SUBLANES - N_HEADS), (0, 0)))
    return dict(
        g=norm_g,
        wq=wq, wk=wk, wv=wv, wf=wf,
        wfm=jnp.concatenate([wq, wk, wv, wqb, wkb, wvb, wf], axis=0),
        bf=jnp.broadcast_to(jnp.pad(b_forget, (0, SUBLANES - N_HEADS))[:, None], (SUBLANES, LANES)),
        wlru=wlru,
        wqb=wqb, wkb=wkb, wvb=wvb,
        cw=lru_conv_w, cb=lru_conv_b.reshape(1, D_LRU),
        wa=_block_diag_halves(lru_wa).astype(BF16), ba=lru_ba.reshape(1, D_LRU),
        wx=_block_diag_halves(lru_wx).astype(BF16), bx=lru_bx.reshape(1, D_LRU),
        lam=lru_lambda.reshape(1, D_LRU),
        rel=rel_bias,
        fcw=ffn_conv_w, fcb=ffn_conv_b.reshape(1, D_FF),
    )


def _prompt_layer(x, p, w, bm, layer, batch, seq):
    zeros8 = jnp.zeros((batch, SUBLANES, D_LRU), F32)
    (qf, kft, vft, kfa, vfa, lft, o_lru, hl, xt, qb, kba, vba, kbt, vbt) = _pre_call(x, zeros8, zeros8, p, batch, seq)
    o_fox = _fox_call(qf, kfa, vfa, batch, seq)
    o_band = _band_call(qb, kba, vba, bm, batch, seq)
    y, gt = _post_call(x, o_fox, o_lru, o_band, jnp.zeros((batch, SUBLANES, D_FF), F32), p, w, layer, batch, seq,
                       TILE)
    state = (kft, vft, lft, kbt, vbt, hl[:, SUBLANES - 1], xt[:, SUBLANES - (LRU_CONV - 1):],
             gt[:, SUBLANES - (FFN_CONV - 1):])
    return y, state


def _sample_layer(x, p, w, bm, layer, batch, t, fox_kt, fox_vt, fox_lf, band_kt, band_vt, lru_h0, lru_conv_h,
                  ffn_conv_h):
    n = batch * t
    (qf, kf, vf, kfn, vfn, lft, xb, yb, qb, kb, vb, kbn, vbn) = _pre_s_call(x, p, batch, t)
    lf_new = jnp.pad(lft.reshape(SUBLANES, batch, t).transpose(1, 0, 2), ((0, 0), (0, 0), (0, LANES - t)))
    o_fox = _fox_s_call(qf, fox_kt, fox_vt, fox_lf, kfn, vfn, lf_new, layer, batch, t)

    hx = jnp.stack([jnp.pad(lru_conv_h[:, j:], ((0, 0), (0, t - (LRU_CONV - 1 - j)), (0, 0))).reshape(n, D_LRU)
                    for j in range(LRU_CONV - 1)])
    h0 = jnp.repeat(lru_h0, t, axis=0)
    o_lru, hl = _lru_s_call(xb, yb, hx, h0, p, t)
    o_band = _band_s_call(qb, band_kt, band_vt, kbn, vbn, bm, layer, batch, t)
    hg = jnp.stack([jnp.pad(ffn_conv_h[:, j:], ((0, 0), (0, t - (FFN_CONV - 1 - j)), (0, 0))).reshape(n, D_FF)
                    for j in range(FFN_CONV - 1)])
    y, gt = _post_call(x, o_fox, o_lru, o_band, hg, p, w, layer, batch, t, n)

    state = (
        kf.reshape(batch, t, N_HEADS, HEAD_DIM),
        vf.reshape(batch, t, N_HEADS, HEAD_DIM),
        lft.reshape(SUBLANES, batch, t)[:N_HEADS].transpose(1, 2, 0),
        kb.reshape(batch, t, N_HEADS, HEAD_DIM),
        vb.reshape(batch, t, N_HEADS, HEAD_DIM),
        hl.reshape(batch, t, D_LRU)[:, t - 1],
        xb.reshape(batch, t, D_LRU)[:, t - (LRU_CONV - 1):],
        gt.reshape(batch, t, D_FF)[:, t - (FFN_CONV - 1):],
    )
    return y, state


def _feature_major(c):
    d, b, s = c.shape[:3]
    return c.transpose(0, 1, 3, 4, 2).reshape(d, b, D_ATT, s)


def _token_major(c):
    d, b, _, s = c.shape
    return c.reshape(d, b, N_HEADS, HEAD_DIM, s).transpose(0, 1, 4, 2, 3)


def kernel(x_prompt, x_sample, cache_fox_k, cache_fox_v, cache_fox_logf, cache_band_k, cache_band_v, state_lru_h, state_lru_conv, state_ffn_conv, norm_g, w_in, b_forget, lru_conv_w, lru_conv_b, lru_wa, lru_ba, lru_wx, lru_bx, lru_lambda, rel_bias, w_out, w_up, ffn_conv_w, ffn_conv_b, w_down):
    bp, seq, _ = x_prompt.shape
    bs, t, _ = x_sample.shape
    depth = norm_g.shape[0]
    assert seq % TILE == 0 and t < LANES and (t & (t - 1)) == 0 and cache_band_k.shape[2] == BAND_KEEP
    assert cache_fox_k.shape[2] % LANES == 0 and bs % SAMPLE_GROUP == 0
    xp = x_prompt.reshape(bp * seq, D_MODEL)
    xs = x_sample.reshape(bs * t, D_MODEL)
    w_in_t = w_in.transpose(0, 2, 1).astype(BF16)
    w = dict(wout=w_out.astype(BF16), wup=w_up.astype(BF16), wdown=w_down.astype(BF16))
    fox_kt, fox_vt = _feature_major(cache_fox_k), _feature_major(cache_fox_v)
    band_kt, band_vt = _feature_major(cache_band_k), _feature_major(cache_band_v)
    fox_lf = cache_fox_logf.transpose(0, 1, 3, 2)
    st_p, st_s = [], []
    for l in range(depth):
        p = _prep_layer(norm_g[l], w_in_t[l], b_forget[l], lru_conv_w[l], lru_conv_b[l], lru_wa[l], lru_ba[l],
                        lru_wx[l], lru_bx[l], lru_lambda[l], rel_bias[l], ffn_conv_w[l], ffn_conv_b[l])
        bm, bm_t = _bias_call(p["rel"])
        xp, new_p = _prompt_layer(xp, p, w, bm_t, l, bp, seq)
        xs, new_s = _sample_layer(xs, p, w, bm, l, bs, t, fox_kt, fox_vt, fox_lf, band_kt, band_vt,
                                  state_lru_h[l], state_lru_conv[l], state_ffn_conv[l])
        st_p.append(new_p)
        st_s.append(new_s)

    stack = lambda states, j: jnp.stack([st[j] for st in states], axis=0)
    outs = [xp.reshape(bp, seq, D_MODEL), xs.reshape(bs, t, D_MODEL)]
    for j in range(8):
        sp, ss = stack(st_p, j), stack(st_s, j)
        if j in (0, 1, 3, 4):
            sp = _token_major(sp)
        elif j == 2:
            sp = sp[:, :, :N_HEADS].transpose(0, 1, 3, 2)
        outs += [sp, ss]
    return tuple(outs)
```
